```python
import jax, jax.numpy as jnp
from jax import lax
import numpy as np

D_MODEL = 4096
BATCH = 2
SEQ = 4096
DEPTH = 1

HEAD_DIM = 128
MIX_WIDTH = D_MODEL
GDN_HEADS = MIX_WIDTH // 2 // HEAD_DIM
ATT_HEADS = MIX_WIDTH // 2 // HEAD_DIM
GDN_WIDTH = GDN_HEADS * HEAD_DIM
ATT_WIDTH = ATT_HEADS * HEAD_DIM
GDN_CONV = 3
GDN_CHUNK = 64
DILATED_PATTERNS = ((128, 1), (512, 4), (2048, 16))
ALIBI_MAX_BIAS = 8.0
FFN_DIM = 256 * ((8 * D_MODEL // 3 + 255) // 256)
FFN_CONV = 3
PLE_DIM = 256
NORM_EPS = 1e-6
NEG_INF = -1e30
IN_COLS = 3 * GDN_WIDTH + GDN_WIDTH + 4 * GDN_HEADS + 3 * ATT_WIDTH

kernel_name = "hymba_gdn_dilated_alibi_convglu_ple"


def rms_norm(x, w):
    xf = x.astype(jnp.float32)
    y = xf * lax.rsqrt(jnp.mean(xf * xf, axis=-1, keepdims=True) + NORM_EPS)
    return (y * w.astype(jnp.float32)).astype(x.dtype)


def l2_norm(x):
    return x * lax.rsqrt(jnp.sum(x * x, axis=-1, keepdims=True) + NORM_EPS)


def depthwise_conv_centred(x, w):
    k_width = w.shape[0]
    half = k_width // 2
    seq = x.shape[1]
    xp = jnp.pad(x, ((0, 0), (half, half), (0, 0)))
    w = w.astype(x.dtype)
    out = xp[:, 0:seq] * w[0]
    for j in range(1, k_width):
        out = out + xp[:, j:j + seq] * w[j]
    return out


def gated_delta_chunked(q, k, v, g, beta):
    bsz, heads, seq, dk = q.shape
    dv = v.shape[-1]
    c = GDN_CHUNK
    nc = seq // c
    q, k, v = (t.reshape(bsz, heads, nc, c, -1) for t in (q, k, v))
    g = jnp.cumsum(g.reshape(bsz, heads, nc, c), axis=-1)
    beta = beta.reshape(bsz, heads, nc, c, 1)
    incl = jnp.tril(jnp.ones((c, c), dtype=bool))
    strict = jnp.tril(jnp.ones((c, c), dtype=bool), -1)
    decay = jnp.where(incl, jnp.exp(jnp.where(incl, g[..., :, None] - g[..., None, :], 0.0)), 0.0)
    k_beta = k * beta
    lower = jnp.where(strict, jnp.einsum('bhnid,bhnjd->bhnij', k_beta, k) * decay, 0.0)
    tmat = lower + jnp.eye(c, dtype=q.dtype)
    u = lax.linalg.triangular_solve(tmat, v * beta, left_side=True, lower=True)
    w = lax.linalg.triangular_solve(tmat, k_beta * jnp.exp(g)[..., None], left_side=True, lower=True)
    intra = jnp.einsum('bhnid,bhnjd->bhnij', q, k) * decay
    q_dec = q * jnp.exp(g)[..., None]
    k_dec = k * jnp.exp(g[..., -1:] - g)[..., None]
    chunk_decay = jnp.exp(g[..., -1])
    xs = tuple(jnp.moveaxis(t, 2, 0) for t in (q_dec, k_dec, u, w, intra, chunk_decay))

    def step(state, inp):
        qd, kd, uc, wc, ic, cd = inp
        v_new = uc - jnp.einsum('bhck,bhkv->bhcv', wc, state)
        out = jnp.einsum('bhck,bhkv->bhcv', qd, state) + jnp.einsum('bhij,bhjv->bhiv', ic, v_new)
        state = state * cd[..., None, None] + jnp.einsum('bhck,bhcv->bhkv', kd, v_new)
        return state, out

    state0 = jnp.zeros((bsz, heads, dk, dv), q.dtype)
    _, out = lax.scan(step, state0, xs)
    return jnp.moveaxis(out, 0, 2).reshape(bsz, heads, seq, dv)


def gdn_mixer(qkv, z, a, b, conv_w, a_log, dt_bias, norm_w):
    bsz, seq, _ = qkv.shape
    qkv = jax.nn.silu(depthwise_conv_centred(qkv, conv_w)).astype(jnp.float32)
    q, k, v = jnp.split(qkv, 3, axis=-1)
    q, k, v = (t.reshape(bsz, seq, GDN_HEADS, HEAD_DIM).transpose(0, 2, 1, 3) for t in (q, k, v))
    q = l2_norm(q) * (HEAD_DIM ** -0.5)
    k = l2_norm(k)
    a = a.astype(jnp.float32).reshape(bsz, seq, 2, GDN_HEADS)
    b = b.astype(jnp.float32).reshape(bsz, seq, 2, GDN_HEADS)
    g = -jnp.exp(a_log.astype(jnp.float32)) * jax.nn.softplus(a + dt_bias.astype(jnp.float32))
    beta = jax.nn.sigmoid(b)
    g = g.transpose(2, 0, 3, 1)
    beta = beta.transpose(2, 0, 3, 1)
    o_fwd = gated_delta_chunked(q, k, v, g[0], beta[0])
    flip = lambda t: jnp.flip(t, axis=2)
    o_bwd = flip(gated_delta_chunked(flip(q), flip(k), flip(v), flip(g[1]), flip(beta[1])))
    o = (o_fwd + o_bwd).transpose(0, 2, 1, 3)
    o = o * lax.rsqrt(jnp.mean(o * o, axis=-1, keepdims=True) + NORM_EPS) * norm_w.astype(jnp.float32)
    o = o * jax.nn.silu(z.astype(jnp.float32).reshape(bsz, seq, GDN_HEADS, HEAD_DIM))
    return o.reshape(bsz, seq, GDN_WIDTH).astype(z.dtype)


def dilated_window_attention(q, k, v, slopes, radius, dilation):
    bsz, seq, heads, hd = q.shape
    n = seq // dilation
    blk = min(radius, n)
    nb = -(-n // blk)
    n_pad = nb * blk

    def to_blocks(t):
        t = t.reshape(bsz, n, dilation, heads, hd).transpose(0, 2, 1, 3, 4)
        t = jnp.pad(t, ((0, 0), (0, 0), (0, n_pad - n), (0, 0), (0, 0)))
        return t.reshape(bsz, dilation, nb, blk, heads, hd)

    def band(t):
        t = jnp.pad(t, ((0, 0), (0, 0), (1, 1), (0, 0), (0, 0), (0, 0)))
        return jnp.concatenate([t[:, :, :-2], t[:, :, 1:-1], t[:, :, 2:]], axis=3)

    qb = to_blocks(q)
    kn, vn = band(to_blocks(k)), band(to_blocks(v))
    s = jnp.einsum('brnqhe,brnkhe->brnhqk', qb, kn, preferred_element_type=jnp.float32)
    qi = (jnp.arange(nb)[:, None] * blk + jnp.arange(blk)[None, :])[:, :, None]
    kj = (jnp.arange(nb)[:, None] * blk - blk + jnp.arange(3 * blk)[None, :])[:, None, :]
    dist = jnp.abs(kj - qi)
    valid = (dist <= radius) & (kj >= 0) & (kj < n)
    bias = -slopes[None, :, None, None] * (dist * dilation).astype(jnp.float32)[:, None]
    s = jnp.where(valid[:, None], s + bias, NEG_INF)
    m = jnp.max(s, axis=-1, keepdims=True)
    e = jnp.exp(s - m)
    den = jnp.sum(e, axis=-1, keepdims=True)
    o = jnp.einsum('brnhqk,brnkhe->brnqhe', e / den, vn.astype(jnp.float32))
    lse = (m + jnp.log(den))[..., 0]
    o = o.reshape(bsz, dilation, n_pad, heads, hd)[:, :, :n]
    o = o.transpose(0, 2, 1, 3, 4).reshape(bsz, seq, heads, hd)
    lse = lse.transpose(0, 1, 2, 4, 3).reshape(bsz, dilation, n_pad, heads)[:, :, :n]
    lse = lse.transpose(0, 2, 1, 3).reshape(bsz, seq, heads)
    return o, lse


def dilated_mixer(qkv):
    bsz, seq, _ = qkv.shape
    q, k, v = jnp.split(qkv, 3, axis=-1)
    q, k, v = (t.reshape(bsz, seq, ATT_HEADS, HEAD_DIM) for t in (q, k, v))
    q = q * (HEAD_DIM ** -0.5)
    slopes = jnp.exp2(-ALIBI_MAX_BIAS * (jnp.arange(ATT_HEADS, dtype=jnp.float32) + 1.0) / ATT_HEADS)
    outs, lses = [], []
    for window, dilation in DILATED_PATTERNS:
        o, lse = dilated_window_attention(q, k, v, slopes, window // (2 * dilation), dilation)
        outs.append(o)
        lses.append(lse)
    wts = jax.nn.softmax(jnp.stack(lses, axis=0), axis=0)
    o = jnp.einsum('gbsh,gbshe->bshe', wts, jnp.stack(outs, axis=0))
    return o.reshape(bsz, seq, ATT_WIDTH).astype(qkv.dtype)


def setup_inputs(seed: int = 0) -> dict:
    key = jax.random.key(seed)
    ks = jax.random.split(key, 20)
    f32 = jnp.float32
    nrm = lambda k, shape, fan: jax.random.normal(k, shape, f32) * (fan ** -0.5)
    gain = lambda k, shape: 1.0 + 0.02 * jax.random.normal(k, shape, f32)
    dt = jnp.exp(jax.random.uniform(ks[5], (DEPTH, 2, GDN_HEADS), f32,
                                    minval=float(np.log(1e-3)), maxval=float(np.log(1e-1))))
    return {
        "x": jax.random.normal(ks[0], (BATCH, SEQ, D_MODEL), f32),
        "p": jax.random.normal(ks[1], (DEPTH, BATCH, SEQ, PLE_DIM), f32),
        "attn_norm": gain(ks[2], (DEPTH, D_MODEL)),
        "w_in": nrm(ks[3], (DEPTH, D_MODEL, IN_COLS), D_MODEL),
        "gdn_conv": nrm(ks[4], (DEPTH, GDN_CONV, 3 * GDN_WIDTH), GDN_CONV),
        "gdn_a_log": jnp.log(jax.random.uniform(ks[6], (DEPTH, 2, GDN_HEADS), f32, minval=1.0, maxval=16.0)),
        "gdn_dt_bias": dt + jnp.log(-jnp.expm1(-dt)),
        "gdn_out_norm": gain(ks[7], (DEPTH, HEAD_DIM)),
        "w_out": nrm(ks[8], (DEPTH, MIX_WIDTH, D_MODEL), MIX_WIDTH),
        "ffn_norm": gain(ks[9], (DEPTH, D_MODEL)),
        "w_up": nrm(ks[10], (DEPTH, D_MODEL, 2 * FFN_DIM), D_MODEL),
        "ffn_conv": nrm(ks[11], (DEPTH, FFN_CONV, 2 * FFN_DIM), FFN_CONV),
        "w_down": nrm(ks[12], (DEPTH, FFN_DIM, D_MODEL), FFN_DIM),
        "ple_norm": gain(ks[13], (DEPTH, D_MODEL)),
        "w_ple_gate": nrm(ks[14], (DEPTH, D_MODEL, D_MODEL), D_MODEL),
        "w_ple_proj": nrm(ks[15], (DEPTH, PLE_DIM, D_MODEL), PLE_DIM),
        "final_norm": gain(ks[16], (D_MODEL,)),
    }


def reference(x, p, attn_norm, w_in, gdn_conv, gdn_a_log, gdn_dt_bias, gdn_out_norm, w_out,
              ffn_norm, w_up, ffn_conv, w_down, ple_norm, w_ple_gate, w_ple_proj, final_norm):
    sizes = [3 * GDN_WIDTH, GDN_WIDTH, 2 * GDN_HEADS, 2 * GDN_HEADS, 3 * ATT_WIDTH]
    split_idx = np.cumsum(sizes)[:-1].tolist()
    h = x
    for i in range(DEPTH):
        u = rms_norm(h, attn_norm[i])
        proj = u @ w_in[i]
        gdn_qkv, gdn_z, gdn_a, gdn_b, att_qkv = jnp.split(proj, split_idx, axis=-1)
        o_gdn = gdn_mixer(gdn_qkv, gdn_z, gdn_a, gdn_b, gdn_conv[i], gdn_a_log[i],
                          gdn_dt_bias[i], gdn_out_norm[i])
        o_att = dilated_mixer(att_qkv)
        h = h + jnp.concatenate([o_gdn, o_att], axis=-1) @ w_out[i]
        u = rms_norm(h, ffn_norm[i])
        gu = depthwise_conv_centred(u @ w_up[i], ffn_conv[i])
        gate, up = jnp.split(gu, 2, axis=-1)
        h = h + (jax.nn.silu(gate) * up) @ w_down[i]
        u = rms_norm(h, ple_norm[i])
        h = h + jax.nn.sigmoid(u @ w_ple_gate[i]) * (p[i] @ w_ple_proj[i])
    return rms_norm(h, final_norm)
```

```python
import functools

import jax
import jax.numpy as jnp
from jax import lax
from jax.experimental import pallas as pl
from jax.experimental.pallas import tpu as pltpu

HEAD_DIM = 128
GDN_HEADS = 16
ATT_HEADS = 16
GDN_WIDTH = GDN_HEADS * HEAD_DIM
ATT_WIDTH = ATT_HEADS * HEAD_DIM
GDN_CHUNK = 64
DILATED_PATTERNS = ((128, 1), (512, 4), (2048, 16))
ALIBI_MAX_BIAS = 8.0
NORM_EPS = 1e-6
NEG_INF = -1e30

LANES = 128
BF16_ROWS = 16
VMEM_LIMIT_BYTES = 56 * 1024 * 1024

_F32 = jnp.float32
_BF16 = jnp.bfloat16


def _params(*semantics):
    return pltpu.CompilerParams(dimension_semantics=semantics, vmem_limit_bytes=VMEM_LIMIT_BYTES)


def _sigmoid(x):
    return 1.0 / (1.0 + jnp.exp(-x))


def _dot(a, b):
    return jnp.dot(a, b, preferred_element_type=_F32)


def _dot_nt(a, b):
    return lax.dot_general(a, b, (((1,), (1,)), ((), ())), preferred_element_type=_F32)


def _rmsnorm_kernel(x_ref, w_ref, o_ref, *, pad_tiles):
    x = x_ref[...].astype(_F32)
    ms = jnp.mean(x * x, axis=-1, keepdims=True)
    y = x * lax.rsqrt(ms + NORM_EPS) * w_ref[...]
    if pad_tiles:
        i = pl.program_id(0)
        inside = jnp.logical_and(i >= pad_tiles, i < pl.num_programs(0) - pad_tiles)
        y = jnp.where(inside, y, 0.0)
    o_ref[...] = y.astype(o_ref.dtype)


def _rmsnorm(x, w, out_dtype, tm=256, pad_tiles=0):
    m, d = x.shape
    nt = m // tm
    return pl.pallas_call(
        functools.partial(_rmsnorm_kernel, pad_tiles=pad_tiles),
        grid=(nt + 2 * pad_tiles,),
        in_specs=[pl.BlockSpec((tm, d), lambda i: (jnp.clip(i - pad_tiles, 0, nt - 1), 0)),
                  pl.BlockSpec((1, d), lambda i: (0, 0))],
        out_specs=pl.BlockSpec((tm, d), lambda i: (i, 0)),
        out_shape=jax.ShapeDtypeStruct((m + 2 * pad_tiles * tm, d), out_dtype),
        compiler_params=_params("arbitrary"),
        name="rmsnorm",
    )(x, w.reshape(1, d).astype(_F32))


def _cast_rows(src_ref, dst_ref, rows=256):
    def body(r, carry):
        sl = pl.ds(pl.multiple_of(r * rows, rows), rows)
        dst_ref[sl, :] = src_ref[sl, :].astype(dst_ref.dtype)
        return carry
    lax.fori_loop(0, src_ref.shape[0] // rows, body, 0)


def _mm_kernel(*refs, n_a, has_res, cast_w):
    a_refs = refs[:n_a]
    w_refs = refs[n_a:2 * n_a]
    pos = 2 * n_a
    res_ref = refs[pos] if has_res else None
    pos += int(has_res)
    o_ref = refs[pos]
    wb_refs = refs[pos + 1:pos + 1 + n_a] if cast_w else w_refs
    if cast_w:
        @pl.when(pl.program_id(1) == 0)
        def _():
            for w_ref, wb_ref in zip(w_refs, wb_refs):
                _cast_rows(w_ref, wb_ref)
    acc = None
    for a_ref, wb_ref in zip(a_refs, wb_refs):
        part = _dot(a_ref[...], wb_ref[...])
        acc = part if acc is None else acc + part
    if has_res:
        acc = acc + res_ref[...]
    o_ref[...] = acc.astype(o_ref.dtype)


def _matmul(a_list, w, *, n_cols, col_off=0, tm, tn, out_dtype, residual=None, name):
    m = a_list[0].shape[0]
    assert col_off % tn == 0 and n_cols % tn == 0 and m % tm == 0
    off = col_off // tn
    cast_w = w.dtype != _BF16
    in_specs, scratch = [], []
    for a in a_list:
        in_specs.append(pl.BlockSpec((tm, a.shape[1]), lambda j, i: (i, 0)))
    row_blk = 0
    for a in a_list:
        kk = a.shape[1]
        assert all(b.shape[1] == kk for b in a_list)
        in_specs.append(pl.BlockSpec((kk, tn), functools.partial(lambda j, i, rb: (rb, j + off), rb=row_blk)))
        if cast_w:
            scratch.append(pltpu.VMEM((kk, tn), _BF16))
        row_blk += 1
    args = list(a_list) + [w] * len(a_list)
    if residual is not None:
        in_specs.append(pl.BlockSpec((tm, tn), lambda j, i: (i, j)))
        args.append(residual)
    return pl.pallas_call(
        functools.partial(_mm_kernel, n_a=len(a_list), has_res=residual is not None, cast_w=cast_w),
        grid=(n_cols // tn, m // tm),
        in_specs=in_specs,
        out_specs=pl.BlockSpec((tm, tn), lambda j, i: (i, j)),
        out_shape=jax.ShapeDtypeStruct((m, n_cols), out_dtype),
        scratch_shapes=scratch,
        compiler_params=_params("arbitrary", "arbitrary"),
        name=name,
    )(*args)


def _gate_kernel(ab_ref, alog_ref, dtb_ref, o_ref):
    x = ab_ref[...]
    rows = x.shape[0]
    xa = x + dtb_ref[...]
    softplus = jnp.maximum(xa, 0.0) + jnp.log(1.0 + jnp.exp(-jnp.abs(xa)))
    g = -jnp.exp(alog_ref[...]) * softplus
    beta = _sigmoid(x)
    c = GDN_CHUNK
    ri = lax.broadcasted_iota(jnp.int32, (c, c), 0)
    ci = lax.broadcasted_iota(jnp.int32, (c, c), 1)
    lower = (ri >= ci).astype(_F32)
    upper = (ri <= ci).astype(_F32)
    lane = lax.broadcasted_iota(jnp.int32, (c, LANES), 1)
    for n in range(rows // c):
        gc = g[n * c:(n + 1) * c]
        prefix = jnp.dot(lower, gc, precision=lax.Precision.HIGHEST, preferred_element_type=_F32)
        suffix = jnp.dot(upper, gc, precision=lax.Precision.HIGHEST, preferred_element_type=_F32)
        out = jnp.where(lane < GDN_HEADS, prefix,
                        jnp.where(lane < 2 * GDN_HEADS, suffix,
                                  jnp.where(lane < 4 * GDN_HEADS, beta[n * c:(n + 1) * c], 0.0)))
        o_ref[n * c:(n + 1) * c, :] = out


def _gdn_gates(ab, a_log, dt_bias, tm=512):
    m = ab.shape[0]
    pad = LANES - 2 * GDN_HEADS
    alog_vec = jnp.pad(a_log.reshape(-1).astype(_F32), (0, pad)).reshape(1, LANES)
    dtb_vec = jnp.pad(dt_bias.reshape(-1).astype(_F32), (0, pad)).reshape(1, LANES)
    vec_spec = pl.BlockSpec((1, LANES), lambda i: (0, 0))
    return pl.pallas_call(
        _gate_kernel,
        grid=(m // tm,),
        in_specs=[pl.BlockSpec((tm, LANES), lambda i: (i, 0)), vec_spec, vec_spec],
        out_specs=pl.BlockSpec((tm, LANES), lambda i: (i, 0)),
        out_shape=jax.ShapeDtypeStruct((m, LANES), _F32),
        compiler_params=_params("arbitrary"),
        name="gdn_gates",
    )(ab, alog_vec, dtb_vec)


def _shift_rows(x, prev_row, next_row):
    rows = x.shape[0]
    row = lax.broadcasted_iota(jnp.int32, x.shape, 0)
    xm1 = jnp.where(row == 0, prev_row, pltpu.roll(x, 1, 0))
    xp1 = jnp.where(row == rows - 1, next_row, pltpu.roll(x, rows - 1, 0))
    return xm1, xp1


def _conv_silu(x_ref, w_ref, dst_ref, *, l2_scale, rows=256):
    seq = x_ref.shape[1]
    n_tiles = seq // rows
    w = w_ref[...]

    def body(r, carry):
        t0 = pl.multiple_of(r * rows, rows)
        x = x_ref[0, pl.ds(t0, rows), :]
        p0 = pl.multiple_of(jnp.maximum(t0 - 8, 0), 8)
        n0 = pl.multiple_of(jnp.minimum(t0 + rows, seq - 8), 8)
        prev_row = jnp.where(r == 0, 0.0, x_ref[0, pl.ds(p0, 8), :][7:8, :])
        next_row = jnp.where(r == n_tiles - 1, 0.0, x_ref[0, pl.ds(n0, 8), :][0:1, :])
        xm1, xp1 = _shift_rows(x, prev_row, next_row)
        y = xm1 * w[0:1, :] + x * w[1:2, :] + xp1 * w[2:3, :]
        y = y * _sigmoid(y)
        if l2_scale is not None:
            y = y * lax.rsqrt(jnp.sum(y * y, axis=-1, keepdims=True) + NORM_EPS) * l2_scale
        dst_ref[pl.ds(t0, rows), :] = y
        return carry

    lax.fori_loop(0, n_tiles, body, 0)


def _unit_tri_inverse(a):
    c = a.shape[0]
    ri = lax.broadcasted_iota(jnp.int32, (c, c), 0)
    ci = lax.broadcasted_iota(jnp.int32, (c, c), 1)
    x = jnp.where(ri == ci, 1.0, 0.0) - a
    p = a
    power = 1
    while 2 * power < c:
        pb = p.astype(_BF16)
        p = _dot(pb, pb)
        x = x + _dot(x.astype(_BF16), p.astype(_BF16))
        power *= 2
    return x


def _gdn_chunk(cidx, state, *, forward, head, qn, kn, vn, slab_ref, grow_ref, out_ref):
    c = GDN_CHUNK
    r0 = pl.multiple_of(cidx * c, c)
    qc = qn[pl.ds(r0, c), :]
    kc = kn[pl.ds(r0, c), :]
    vc = vn[pl.ds(r0, c), :]
    slab = slab_ref[0, pl.ds(r0, c), :]
    lane = lax.broadcasted_iota(jnp.int32, slab.shape, 1)
    g_lane = head if forward else GDN_HEADS + head
    b_lane = 2 * GDN_HEADS + g_lane
    gcol = jnp.sum(jnp.where(lane == g_lane, slab, 0.0), axis=-1, keepdims=True)
    bcol = jnp.sum(jnp.where(lane == b_lane, slab, 0.0), axis=-1, keepdims=True)
    grow = grow_ref[0, 0, pl.ds(cidx, 1), :]
    ri = lax.broadcasted_iota(jnp.int32, (c, c), 0)
    ci = lax.broadcasted_iota(jnp.int32, (c, c), 1)
    incl = (ri >= ci) if forward else (ri <= ci)
    strict = (ri > ci) if forward else (ri < ci)
    decay = jnp.where(incl, jnp.exp(jnp.where(incl, gcol - grow, 0.0)), 0.0)
    kb = kc * bcol
    k16 = kc.astype(_BF16)
    a = jnp.where(strict, _dot_nt(kb.astype(_BF16), k16) * decay, 0.0)
    tinv = _unit_tri_inverse(a)
    eg = jnp.exp(gcol)
    rhs = jnp.concatenate([vc * bcol, kb * eg], axis=1).astype(_BF16)
    uw = _dot(tinv.astype(_BF16), rhs)
    u = uw[:, :HEAD_DIM]
    w = uw[:, HEAD_DIM:]
    intra = jnp.where(incl, _dot_nt(qc.astype(_BF16), k16) * decay, 0.0)
    glast = gcol[c - 1:c, :] if forward else gcol[0:1, :]
    q_dec = qc * eg
    k_dec = kc * jnp.exp(glast - gcol)
    s16 = state.astype(_BF16)
    v_new = u - _dot(w.astype(_BF16), s16)
    v16 = v_new.astype(_BF16)
    out = _dot(q_dec.astype(_BF16), s16) + _dot(intra.astype(_BF16), v16)
    out_ref[pl.ds(r0, c), :] = out
    return state * jnp.exp(glast) + _dot(k_dec.T.astype(_BF16), v16)


def _gdn_kernel(q_ref, k_ref, v_ref, cwq_ref, cwk_ref, cwv_ref, slab_ref, gf_ref, gb_ref, z_ref, nw_ref,
                o_ref, qn, kn, vn, of, ob):
    head = pl.program_id(1)
    seq = q_ref.shape[1]
    n_chunks = seq // GDN_CHUNK
    _conv_silu(q_ref, cwq_ref, qn, l2_scale=HEAD_DIM ** -0.5)
    _conv_silu(k_ref, cwk_ref, kn, l2_scale=1.0)
    _conv_silu(v_ref, cwv_ref, vn, l2_scale=None)

    common = dict(head=head, qn=qn, kn=kn, vn=vn, slab_ref=slab_ref)

    def step(n, carry):
        sf, sb = carry
        sf = _gdn_chunk(n, sf, forward=True, grow_ref=gf_ref, out_ref=of, **common)
        sb = _gdn_chunk(n_chunks - 1 - n, sb, forward=False, grow_ref=gb_ref, out_ref=ob, **common)
        return sf, sb

    zero = jnp.zeros((HEAD_DIM, HEAD_DIM), _F32)
    lax.fori_loop(0, n_chunks, step, (zero, zero))

    rows = 256

    def finish(r, carry):
        sl = pl.ds(pl.multiple_of(r * rows, rows), rows)
        o = of[sl, :] + ob[sl, :]
        o = o * lax.rsqrt(jnp.mean(o * o, axis=-1, keepdims=True) + NORM_EPS) * nw_ref[...]
        z = z_ref[0, sl, :].astype(_F32)
        o_ref[0, sl, :] = (o * (z * _sigmoid(z))).astype(o_ref.dtype)
        return carry

    lax.fori_loop(0, seq // rows, finish, 0)


def _gdn_mixer(qkv, z, slab, slab_t, conv_w, norm_w):
    bsz, seq, _ = qkv.shape
    nc = seq // GDN_CHUNK
    h = GDN_HEADS
    col = lambda off: pl.BlockSpec((1, seq, HEAD_DIM), functools.partial(lambda b, hh, o: (b, 0, o + hh), o=off))
    cw = lambda off: pl.BlockSpec((3, HEAD_DIM), functools.partial(lambda b, hh, o: (0, o + hh), o=off))
    grow = lambda off: pl.BlockSpec((1, 1, nc, GDN_CHUNK),
                                    functools.partial(lambda b, hh, o: (b, o + hh, 0, 0), o=off))
    return pl.pallas_call(
        _gdn_kernel,
        grid=(bsz, h),
        in_specs=[col(0), col(h), col(2 * h), cw(0), cw(h), cw(2 * h),
                  pl.BlockSpec((1, seq, LANES), lambda b, hh: (b, 0, 0)),
                  grow(0), grow(h),
                  col(0),
                  pl.BlockSpec((1, HEAD_DIM), lambda b, hh: (0, 0))],
        out_specs=col(0),
        out_shape=jax.ShapeDtypeStruct((bsz, seq, GDN_WIDTH), _BF16),
        scratch_shapes=[pltpu.VMEM((seq, HEAD_DIM), _F32)] * 5,
        compiler_params=_params("arbitrary", "arbitrary"),
        name="gdn_mixer",
    )(qkv, qkv, qkv, conv_w, conv_w, conv_w, slab, slab_t, slab_t, z, norm_w.reshape(1, HEAD_DIM).astype(_F32))


def _attn_kernel(slope_ref, q_ref, k_ref, v_ref, o_ref, lse_ref, *, dilation, radius, bq):
    n = q_ref.shape[1]
    width = bq + 2 * radius
    slope = slope_ref[0][:, 0:1] * float(dilation)
    scale = HEAD_DIM ** -0.5

    def body(i, carry):
        i0 = pl.multiple_of(i * bq, bq)
        ws = pl.multiple_of(jnp.clip(i0 - radius, 0, n - width), radius)
        q = q_ref[0, pl.ds(i0, bq), :]
        kw = k_ref[0, pl.ds(ws, width), :]
        vw = v_ref[0, pl.ds(ws, width), :]
        s = _dot_nt(q, kw) * scale
        qi = i0 + lax.broadcasted_iota(jnp.int32, (bq, width), 0)
        kj = ws + lax.broadcasted_iota(jnp.int32, (bq, width), 1)
        dist = jnp.abs(kj - qi)
        s = jnp.where(dist <= radius, s - slope * dist.astype(_F32), NEG_INF)
        m = jnp.max(s, axis=-1, keepdims=True)
        e = jnp.exp(s - m)
        den = jnp.sum(e, axis=-1, keepdims=True)
        o = _dot(e.astype(_BF16), vw) / den
        o_ref[0, pl.ds(i0, bq), :] = o
        lse_ref[0, pl.ds(i0, bq), :] = jnp.broadcast_to(m + jnp.log(den), (bq, HEAD_DIM))
        return carry

    lax.fori_loop(0, n // bq, body, 0)


def _dilated_attention(qkv, slopes, window, dilation, bq=128):
    bsz, seq, _ = qkv.shape
    radius = window // (2 * dilation)
    n = seq // dilation
    h = ATT_HEADS
    assert n % bq == 0 and n >= bq + 2 * radius and bq % radius == 0
    view = qkv.reshape(bsz, n, dilation * 3 * ATT_WIDTH)
    col = lambda off: pl.BlockSpec((1, n, HEAD_DIM),
                                   functools.partial(lambda b, hh, r, o: (b, 0, r * 3 * h + o + hh), o=off))
    out_spec = pl.BlockSpec((1, n, HEAD_DIM), lambda b, hh, r: (b, 0, r * h + hh))
    out_shape = jax.ShapeDtypeStruct((bsz, n, dilation * ATT_WIDTH), _F32)
    o, lse = pl.pallas_call(
        functools.partial(_attn_kernel, dilation=dilation, radius=radius, bq=bq),
        grid=(bsz, h, dilation),
        in_specs=[pl.BlockSpec((1, 1, LANES), lambda b, hh, r: (hh, 0, 0)), col(0), col(h), col(2 * h)],
        out_specs=[out_spec, out_spec],
        out_shape=[out_shape, out_shape],
        compiler_params=_params("arbitrary", "arbitrary", "arbitrary"),
        name=f"dilated_attn_d{dilation}",
    )(slopes, view, view, view)
    return o.reshape(bsz * seq, ATT_WIDTH), lse.reshape(bsz * seq, ATT_WIDTH)


def _combine_kernel(*refs):
    n = (len(refs) - 1) // 2
    o_refs, l_refs, out_ref = refs[:n], refs[n:2 * n], refs[2 * n]
    lses = [r[...] for r in l_refs]
    m = functools.reduce(jnp.maximum, lses)
    ws = [jnp.exp(l - m) for l in lses]
    num = sum(w * r[...] for w, r in zip(ws, o_refs))
    out_ref[...] = (num / sum(ws)).astype(out_ref.dtype)


def _combine_patterns(outs, lses, tm=256):
    m, d = outs[0].shape
    spec = pl.BlockSpec((tm, d), lambda i: (i, 0))
    return pl.pallas_call(
        _combine_kernel,
        grid=(m // tm,),
        in_specs=[spec] * (2 * len(outs)),
        out_specs=spec,
        out_shape=jax.ShapeDtypeStruct((m, d), _BF16),
        compiler_params=_params("arbitrary"),
        name="combine_patterns",
    )(*outs, *lses)


def _ffn_up_kernel(a_ref, wg_ref, wu_ref, cg_ref, cu_ref, o_ref, wcat, *, tiles_per_seq, halo):
    i = pl.program_id(1)
    tn = wg_ref.shape[1]
    tm = o_ref.shape[0]

    @pl.when(i == 0)
    def _():
        _cast_rows(wg_ref, wcat.at[:, pl.ds(0, tn)])
        _cast_rows(wu_ref, wcat.at[:, pl.ds(tn, tn)])

    y_all = _dot(a_ref[...], wcat[...])
    y = y_all[halo:halo + tm]
    first = (i % tiles_per_seq) == 0
    last = (i % tiles_per_seq) == tiles_per_seq - 1
    prev_row = jnp.where(first, 0.0, y_all[halo - 1:halo])
    next_row = jnp.where(last, 0.0, y_all[halo + tm:halo + tm + 1])
    ym1, yp1 = _shift_rows(y, prev_row, next_row)
    cw = jnp.concatenate([cg_ref[...], cu_ref[...]], axis=1)
    conv = ym1 * cw[0:1, :] + y * cw[1:2, :] + yp1 * cw[2:3, :]
    gate = conv[:, :tn]
    up = conv[:, tn:]
    o_ref[...] = (gate * _sigmoid(gate) * up).astype(o_ref.dtype)


def _ffn_up(u_pad, w_up, conv_w, seq, pad_rows, tm=512, tn=256):
    m, k = u_pad.shape[0] - 2 * pad_rows, u_pad.shape[1]
    ffn = w_up.shape[1] // 2
    halo = BF16_ROWS
    assert ffn % tn == 0 and seq % tm == 0 and pad_rows >= halo
    nj = ffn // tn
    assert tm % halo == 0 and pad_rows % halo == 0
    a_spec = pl.BlockSpec((pl.Element(tm + 2 * halo), pl.Element(k)),
                          lambda j, i: ((i * (tm // halo) + (pad_rows // halo - 1)) * halo, 0))
    return pl.pallas_call(
        functools.partial(_ffn_up_kernel, tiles_per_seq=seq // tm, halo=halo),
        grid=(nj, m // tm),
        in_specs=[a_spec,
                  pl.BlockSpec((k, tn), lambda j, i: (0, j)),
                  pl.BlockSpec((k, tn), lambda j, i: (0, j + nj)),
                  pl.BlockSpec((3, tn), lambda j, i: (0, j)),
                  pl.BlockSpec((3, tn), lambda j, i: (0, j + nj))],
        out_specs=pl.BlockSpec((tm, tn), lambda j, i: (i, j)),
        out_shape=jax.ShapeDtypeStruct((m, ffn), _BF16),
        scratch_shapes=[pltpu.VMEM((k, 2 * tn), _BF16)],
        compiler_params=_params("arbitrary", "arbitrary"),
        name="ffn_up_conv_glu",
    )(u_pad, w_up, w_up, conv_w, conv_w)


def _ple_kernel(a_ref, wg_ref, p_ref, wp_ref, h_ref, o_ref, wgb, wpb):
    @pl.when(pl.program_id(1) == 0)
    def _():
        _cast_rows(wg_ref, wgb)
        _cast_rows(wp_ref, wpb)

    gate = _sigmoid(_dot(a_ref[...], wgb[...]))
    emb = _dot(p_ref[...], wpb[...])
    o_ref[...] = h_ref[...] + gate * emb


def _ple(u, w_gate, p, w_proj, h, tm=512, tn=512):
    m, k = u.shape
    n = w_gate.shape[1]
    kp = p.shape[1]
    return pl.pallas_call(
        _ple_kernel,
        grid=(n // tn, m // tm),
        in_specs=[pl.BlockSpec((tm, k), lambda j, i: (i, 0)),
                  pl.BlockSpec((k, tn), lambda j, i: (0, j)),
                  pl.BlockSpec((tm, kp), lambda j, i: (i, 0)),
                  pl.BlockSpec((kp, tn), lambda j, i: (0, j)),
                  pl.BlockSpec((tm, tn), lambda j, i: (i, j))],
        out_specs=pl.BlockSpec((tm, tn), lambda j, i: (i, j)),
        out_shape=jax.ShapeDtypeStruct((m, n), _F32),
        scratch_shapes=[pltpu.VMEM((k, tn), _BF16), pltpu.VMEM((kp, tn), _BF16)],
        compiler_params=_params("arbitrary", "arbitrary"),
        name="ple_gate",
    )(u, w_gate, p, w_proj, h)


def kernel(x, p, attn_norm, w_in, gdn_conv, gdn_a_log, gdn_dt_bias, gdn_out_norm, w_out, ffn_norm, w_up,
           ffn_conv, w_down, ple_norm, w_ple_gate, w_ple_proj, final_norm):
    bsz, seq, d_model = x.shape
    depth = w_in.shape[0]
    m = bsz * seq
    qkv_cols = 3 * GDN_WIDTH
    ab_off = qkv_cols + GDN_WIDTH
    att_off = ab_off + 4 * GDN_HEADS
    slopes = jnp.exp2(-ALIBI_MAX_BIAS * (jnp.arange(ATT_HEADS, dtype=_F32) + 1.0) / ATT_HEADS)
    slopes = jnp.broadcast_to(slopes[:, None, None], (ATT_HEADS, 1, LANES))

    h = x.reshape(m, d_model)
    for i in range(depth):
        u = _rmsnorm(h, attn_norm[i], _BF16)
        wi = w_in[i]
        gdn_qkv = _matmul([u], wi, n_cols=qkv_cols, tm=512, tn=512, out_dtype=_F32, name="proj_gdn_qkv")
        gdn_z = _matmul([u], wi, n_cols=GDN_WIDTH, col_off=qkv_cols, tm=512, tn=512, out_dtype=_BF16,
                        name="proj_gdn_z")
        ab = _matmul([u], wi, n_cols=LANES, col_off=ab_off, tm=512, tn=LANES, out_dtype=_F32, name="proj_gdn_ab")
        att_qkv = _matmul([u], wi[:, att_off:], n_cols=3 * ATT_WIDTH, tm=512, tn=512, out_dtype=_BF16,
                          name="proj_att_qkv")

        slab = _gdn_gates(ab, gdn_a_log[i], gdn_dt_bias[i]).reshape(bsz, seq, LANES)
        slab_t = slab.transpose(0, 2, 1).reshape(bsz, LANES, seq // GDN_CHUNK, GDN_CHUNK)
        o_gdn = _gdn_mixer(gdn_qkv.reshape(bsz, seq, qkv_cols), gdn_z.reshape(bsz, seq, GDN_WIDTH), slab, slab_t,
                           gdn_conv[i], gdn_out_norm[i]).reshape(m, GDN_WIDTH)

        att3 = att_qkv.reshape(bsz, seq, 3 * ATT_WIDTH)
        outs, lses = zip(*[_dilated_attention(att3, slopes, window, dilation)
                           for window, dilation in DILATED_PATTERNS])
        o_att = _combine_patterns(outs, lses)

        h = _matmul([o_gdn, o_att], w_out[i], n_cols=d_model, tm=512, tn=512, out_dtype=_F32, residual=h,
                    name="out_proj")

        norm_tm = 256
        u_pad = _rmsnorm(h, ffn_norm[i], _BF16, tm=norm_tm, pad_tiles=1)
        act = _ffn_up(u_pad, w_up[i], ffn_conv[i], seq, pad_rows=norm_tm)
        h = _matmul([act], w_down[i].astype(_BF16), n_cols=d_model, tm=512, tn=512, out_dtype=_F32, residual=h,
                    name="ffn_down")

        u = _rmsnorm(h, ple_norm[i], _BF16)
        h = _ple(u, w_ple_gate[i], p[i].reshape(m, -1).astype(_BF16), w_ple_proj[i], h)
    return _rmsnorm(h, final_norm, x.dtype).reshape(bsz, seq, d_model)
```

```python
import functools

import jax
import jax.numpy as jnp
from jax import lax
from jax.experimental import pallas as pl
from jax.experimental.pallas import tpu as pltpu

HEAD_DIM = 128
GDN_HEADS = 16
ATT_HEADS = 16
GDN_WIDTH = GDN_HEADS * HEAD_DIM
ATT_WIDTH = ATT_HEADS * HEAD_DIM
GDN_CHUNK = 64
DILATED_PATTERNS = ((128, 1), (512, 4), (2048, 16))
ALIBI_MAX_BIAS = 8.0
NORM_EPS = 1e-6
NEG_INF = -1e30

LANES = 128
BF16_ROWS = 16
VMEM_LIMIT_BYTES = 56 * 1024 * 1024

_F32 = jnp.float32
_BF16 = jnp.bfloat16


def _params(*semantics):
    return pltpu.CompilerParams(dimension_semantics=semantics, vmem_limit_bytes=VMEM_LIMIT_BYTES)


def _sigmoid(x):
    return 1.0 / (1.0 + jnp.exp(-x))


def _dot(a, b):
    return jnp.dot(a, b, preferred_element_type=_F32)


def _dot_nt(a, b):
    return lax.dot_general(a, b, (((1,), (1,)), ((), ())), preferred_element_type=_F32)


def _rmsnorm_kernel(x_ref, w_ref, o_ref, *, pad_tiles):
    x = x_ref[...].astype(_F32)
    ms = jnp.mean(x * x, axis=-1, keepdims=True)
    y = x * lax.rsqrt(ms + NORM_EPS) * w_ref[...]
    if pad_tiles:
        i = pl.program_id(0)
        inside = jnp.logical_and(i >= pad_tiles, i < pl.num_programs(0) - pad_tiles)
        y = jnp.where(inside, y, 0.0)
    o_ref[...] = y.astype(o_ref.dtype)


def _rmsnorm(x, w, out_dtype, tm=256, pad_tiles=0):
    m, d = x.shape
    nt = m // tm
    return pl.pallas_call(
        functools.partial(_rmsnorm_kernel, pad_tiles=pad_tiles),
        grid=(nt + 2 * pad_tiles,),
        in_specs=[pl.BlockSpec((tm, d), lambda i: (jnp.clip(i - pad_tiles, 0, nt - 1), 0)),
                  pl.BlockSpec((1, d), lambda i: (0, 0))],
        out_specs=pl.BlockSpec((tm, d), lambda i: (i, 0)),
        out_shape=jax.ShapeDtypeStruct((m + 2 * pad_tiles * tm, d), out_dtype),
        compiler_params=_params("arbitrary"),
        name="rmsnorm",
    )(x, w.reshape(1, d).astype(_F32))


def _cast_rows(src_ref, dst_ref, rows=256):
    def body(r, carry):
        sl = pl.ds(pl.multiple_of(r * rows, rows), rows)
        dst_ref[sl, :] = src_ref[sl, :].astype(dst_ref.dtype)
        return carry
    lax.fori_loop(0, src_ref.shape[0] // rows, body, 0)


def _mm_kernel(*refs, n_a, has_res, cast_w):
    a_refs = refs[:n_a]
    w_refs = refs[n_a:2 * n_a]
    pos = 2 * n_a
    res_ref = refs[pos] if has_res else None
    pos += int(has_res)
    o_ref = refs[pos]
    wb_refs = refs[pos + 1:pos + 1 + n_a] if cast_w else w_refs
    if cast_w:
        @pl.when(pl.program_id(1) == 0)
        def _():
            for w_ref, wb_ref in zip(w_refs, wb_refs):
                _cast_rows(w_ref, wb_ref)
    acc = None
    for a_ref, wb_ref in zip(a_refs, wb_refs):
        part = _dot(a_ref[...], wb_ref[...])
        acc = part if acc is None else acc + part
    if has_res:
        acc = acc + res_ref[...]
    o_ref[...] = acc.astype(o_ref.dtype)


def _matmul(a_list, w, *, n_cols, col_off=0, tm, tn, out_dtype, residual=None, name):
    m = a_list[0].shape[0]
    assert col_off % tn == 0 and n_cols % tn == 0 and m % tm == 0
    off = col_off // tn
    cast_w = w.dtype != _BF16
    in_specs, scratch = [], []
    for a in a_list:
        in_specs.append(pl.BlockSpec((tm, a.shape[1]), lambda j, i: (i, 0)))
    row_blk = 0
    for a in a_list:
        kk = a.shape[1]
        assert all(b.shape[1] == kk for b in a_list)
        in_specs.append(pl.BlockSpec((kk, tn), functools.partial(lambda j, i, rb: (rb, j + off), rb=row_blk)))
        if cast_w:
            scratch.append(pltpu.VMEM((kk, tn), _BF16))
        row_blk += 1
    args = list(a_list) + [w] * len(a_list)
    if residual is not None:
        in_specs.append(pl.BlockSpec((tm, tn), lambda j, i: (i, j)))
        args.append(residual)
    return pl.pallas_call(
        functools.partial(_mm_kernel, n_a=len(a_list), has_res=residual is not None, cast_w=cast_w),
        grid=(n_cols // tn, m // tm),
        in_specs=in_specs,
        out_specs=pl.BlockSpec((tm, tn), lambda j, i: (i, j)),
        out_shape=jax.ShapeDtypeStruct((m, n_cols), out_dtype),
        scratch_shapes=scratch,
        compiler_params=_params("arbitrary", "arbitrary"),
        name=name,
    )(*args)


def _gate_kernel(ab_ref, alog_ref, dtb_ref, o_ref):
    x = ab_ref[...]
    rows = x.shape[0]
    xa = x + dtb_ref[...]
    softplus = jnp.maximum(xa, 0.0) + jnp.log(1.0 + jnp.exp(-jnp.abs(xa)))
    g = -jnp.exp(alog_ref[...]) * softplus
    beta = _sigmoid(x)
    c = GDN_CHUNK
    ri = lax.broadcasted_iota(jnp.int32, (c, c), 0)
    ci = lax.broadcasted_iota(jnp.int32, (c, c), 1)
    lower = (ri >= ci).astype(_F32)
    upper = (ri <= ci).astype(_F32)
    lane = lax.broadcasted_iota(jnp.int32, (c, LANES), 1)
    for n in range(rows // c):
        gc = g[n * c:(n + 1) * c]
        prefix = jnp.dot(lower, gc, precision=lax.Precision.HIGHEST, preferred_element_type=_F32)
        suffix = jnp.dot(upper, gc, precision=lax.Precision.HIGHEST, preferred_element_type=_F32)
        out = jnp.where(lane < GDN_HEADS, prefix,
                        jnp.where(lane < 2 * GDN_HEADS, suffix,
                                  jnp.where(lane < 4 * GDN_HEADS, beta[n * c:(n + 1) * c], 0.0)))
        o_ref[n * c:(n + 1) * c, :] = out


def _gdn_gates(ab, a_log, dt_bias, tm=512):
    m = ab.shape[0]
    pad = LANES - 2 * GDN_HEADS
    alog_vec = jnp.pad(a_log.reshape(-1).astype(_F32), (0, pad)).reshape(1, LANES)
    dtb_vec = jnp.pad(dt_bias.reshape(-1).astype(_F32), (0, pad)).reshape(1, LANES)
    vec_spec = pl.BlockSpec((1, LANES), lambda i: (0, 0))
    return pl.pallas_call(
        _gate_kernel,
        grid=(m // tm,),
        in_specs=[pl.BlockSpec((tm, LANES), lambda i: (i, 0)), vec_spec, vec_spec],
        out_specs=pl.BlockSpec((tm, LANES), lambda i: (i, 0)),
        out_shape=jax.ShapeDtypeStruct((m, LANES), _F32),
        compiler_params=_params("arbitrary"),
        name="gdn_gates",
    )(ab, alog_vec, dtb_vec)


def _shift_rows(x, prev_row, next_row):
    rows = x.shape[0]
    row = lax.broadcasted_iota(jnp.int32, x.shape, 0)
    xm1 = jnp.where(row == 0, prev_row, pltpu.roll(x, 1, 0))
    xp1 = jnp.where(row == rows - 1, next_row, pltpu.roll(x, rows - 1, 0))
    return xm1, xp1


def _conv_silu(x_ref, w_ref, dst_ref, *, l2_scale, rows=256):
    seq = x_ref.shape[1]
    n_tiles = seq // rows
    w = w_ref[...]

    def body(r, carry):
        t0 = pl.multiple_of(r * rows, rows)
        x = x_ref[0, pl.ds(t0, rows), :]
        p0 = pl.multiple_of(jnp.maximum(t0 - 8, 0), 8)
        n0 = pl.multiple_of(jnp.minimum(t0 + rows, seq - 8), 8)
        prev_row = jnp.where(r == 0, 0.0, x_ref[0, pl.ds(p0, 8), :][7:8, :])
        next_row = jnp.where(r == n_tiles - 1, 0.0, x_ref[0, pl.ds(n0, 8), :][0:1, :])
        xm1, xp1 = _shift_rows(x, prev_row, next_row)
        y = xm1 * w[0:1, :] + x * w[1:2, :] + xp1 * w[2:3, :]
        y = y * _sigmoid(y)
        if l2_scale is not None:
            y = y * lax.rsqrt(jnp.sum(y * y, axis=-1, keepdims=True) + NORM_EPS) * l2_scale
        dst_ref[pl.ds(t0, rows), :] = y
        return carry

    lax.fori_loop(0, n_tiles, body, 0)


def _unit_tri_inverses(mats):
    c = mats[0].shape[0]
    ri = lax.broadcasted_iota(jnp.int32, (c, c), 0)
    ci = lax.broadcasted_iota(jnp.int32, (c, c), 1)
    eye = jnp.where(ri == ci, 1.0, 0.0)
    xs = [eye - a for a in mats]
    ps = [_dot(a.astype(_BF16), a.astype(_BF16)) for a in mats]
    power = 2
    while 2 * power < c:
        xps = [_dot(jnp.concatenate([x, p], axis=0).astype(_BF16), p.astype(_BF16)) for x, p in zip(xs, ps)]
        xs = [x + xp[:c] for x, xp in zip(xs, xps)]
        ps = [xp[c:] for xp in xps]
        power *= 2
    return [x + _dot(x.astype(_BF16), p.astype(_BF16)) for x, p in zip(xs, ps)]


def _gdn_prepare_chunks(cidxs, *, head, qn, kn, vn, slab_ref, grow_refs, dir_refs):
    c = GDN_CHUNK
    ri = lax.broadcasted_iota(jnp.int32, (c, c), 0)
    ci = lax.broadcasted_iota(jnp.int32, (c, c), 1)
    lane = lax.broadcasted_iota(jnp.int32, (c, LANES), 1)
    chunks = []
    for cidx in cidxs:
        r0 = pl.multiple_of(cidx * c, c)
        qc = qn[pl.ds(r0, c), :]
        kc = kn[pl.ds(r0, c), :]
        prod = _dot_nt(jnp.concatenate([qc, kc], axis=0).astype(_BF16), kc.astype(_BF16))
        chunks.append(dict(cidx=cidx, r0=r0, qc=qc, kc=kc, vc=vn[pl.ds(r0, c), :], qk=prod[:c], kk=prod[c:],
                           slab=slab_ref[0, pl.ds(r0, c), :]))
    chains = []
    for ch in chunks:
        for forward, grow_ref, refs in zip((True, False), grow_refs, dir_refs):
            g_lane = head if forward else GDN_HEADS + head
            b_lane = 2 * GDN_HEADS + g_lane
            gcol = jnp.sum(jnp.where(lane == g_lane, ch["slab"], 0.0), axis=-1, keepdims=True)
            bcol = jnp.sum(jnp.where(lane == b_lane, ch["slab"], 0.0), axis=-1, keepdims=True)
            grow = grow_ref[0, 0, pl.ds(ch["cidx"], 1), :]
            incl = (ri >= ci) if forward else (ri <= ci)
            strict = (ri > ci) if forward else (ri < ci)
            decay = jnp.where(incl, jnp.exp(jnp.where(incl, gcol - grow, 0.0)), 0.0)
            chains.append(dict(ch=ch, refs=refs, gcol=gcol, bcol=bcol, incl=incl, decay=decay,
                               glast=gcol[c - 1:c, :] if forward else gcol[0:1, :],
                               a=jnp.where(strict, ch["kk"] * decay * bcol, 0.0)))
    tinvs = _unit_tri_inverses([cn["a"] for cn in chains])
    uws = []
    for cn, tinv in zip(chains, tinvs):
        ch = cn["ch"]
        cn["eg"] = jnp.exp(cn["gcol"])
        rhs = jnp.concatenate([ch["vc"] * cn["bcol"], ch["kc"] * cn["bcol"] * cn["eg"]], axis=1).astype(_BF16)
        uws.append(_dot(tinv.astype(_BF16), rhs))
    for cn, uw in zip(chains, uws):
        ch = cn["ch"]
        u_ref, wq_ref, kdt_ref, intra_ref, dec_ref = cn["refs"]
        r0 = ch["r0"]
        r1 = pl.multiple_of(ch["cidx"] * 2 * c, 2 * c)
        u_ref[pl.ds(r0, c), :] = uw[:, :HEAD_DIM]
        wq_ref[pl.ds(r1, c), :] = uw[:, HEAD_DIM:].astype(_BF16)
        wq_ref[pl.ds(r1 + c, c), :] = (ch["qc"] * cn["eg"]).astype(_BF16)
        kdt_ref[pl.ds(r1, 2 * c), :] = (ch["kc"] * jnp.exp(cn["glast"] - cn["gcol"])).T.astype(_BF16)
        intra_ref[pl.ds(r0, c), :] = jnp.where(cn["incl"], ch["qk"] * cn["decay"], 0.0).astype(_BF16)
        dec_ref[pl.ds(ch["cidx"], 1), :] = jnp.broadcast_to(jnp.exp(cn["glast"]), (1, HEAD_DIM))


def _gdn_scan_chunks(cidxs, states, dir_refs, out_refs):
    c = GDN_CHUNK
    r0s = [pl.multiple_of(cidx * c, c) for cidx in cidxs]
    r1s = [pl.multiple_of(cidx * 2 * c, 2 * c) for cidx in cidxs]
    ws_qs = [_dot(refs[1][pl.ds(r1, 2 * c), :], s.astype(_BF16)) for refs, r1, s in zip(dir_refs, r1s, states)]
    v16s = [(refs[0][pl.ds(r0, c), :] - wq[:c]).astype(_BF16) for refs, r0, wq in zip(dir_refs, r0s, ws_qs)]
    new_states = [s * refs[4][pl.ds(cidx, 1), :] + _dot(refs[2][pl.ds(r1, 2 * c), :], v16)
                  for s, refs, cidx, r1, v16 in zip(states, dir_refs, cidxs, r1s, v16s)]
    for refs, out_ref, r0, wq, v16 in zip(dir_refs, out_refs, r0s, ws_qs, v16s):
        out_ref[pl.ds(r0, c), :] = wq[c:] + _dot(refs[3][pl.ds(r0, c), :], v16)
    return new_states


def _gdn_kernel(q_ref, k_ref, v_ref, cwq_ref, cwk_ref, cwv_ref, slab_ref, gf_ref, gb_ref, z_ref, nw_ref,
                o_ref, qn, kn, vn, of, ob, uf, ub, wqf, wqb, kdtf, kdtb, intraf, intrab, decf, decb):
    head = pl.program_id(1)
    seq = q_ref.shape[1]
    n_chunks = seq // GDN_CHUNK
    _conv_silu(q_ref, cwq_ref, qn, l2_scale=HEAD_DIM ** -0.5)
    _conv_silu(k_ref, cwk_ref, kn, l2_scale=1.0)
    _conv_silu(v_ref, cwv_ref, vn, l2_scale=None)

    fwd_refs = (uf, wqf, kdtf, intraf, decf)
    bwd_refs = (ub, wqb, kdtb, intrab, decb)
    group = 8

    def prepare(n, carry):
        _gdn_prepare_chunks([n * group + j for j in range(group)], head=head, qn=qn, kn=kn, vn=vn,
                            slab_ref=slab_ref, grow_refs=(gf_ref, gb_ref), dir_refs=(fwd_refs, bwd_refs))
        return carry

    lax.fori_loop(0, n_chunks // group, prepare, 0)

    def scan(n, carry):
        return tuple(_gdn_scan_chunks([n, n_chunks - 1 - n], carry, (fwd_refs, bwd_refs), (of, ob)))

    zero = jnp.zeros((HEAD_DIM, HEAD_DIM), _F32)
    lax.fori_loop(0, n_chunks, scan, (zero, zero))

    rows = 256

    def finish(r, carry):
        sl = pl.ds(pl.multiple_of(r * rows, rows), rows)
        o = of[sl, :] + ob[sl, :]
        o = o * lax.rsqrt(jnp.mean(o * o, axis=-1, keepdims=True) + NORM_EPS) * nw_ref[...]
        z = z_ref[0, sl, :].astype(_F32)
        o_ref[0, sl, :] = (o * (z * _sigmoid(z))).astype(o_ref.dtype)
        return carry

    lax.fori_loop(0, seq // rows, finish, 0)


def _gdn_mixer(qkv, z, slab, slab_t, conv_w, norm_w):
    bsz, seq, _ = qkv.shape
    nc = seq // GDN_CHUNK
    h = GDN_HEADS
    col = lambda off: pl.BlockSpec((1, seq, HEAD_DIM), functools.partial(lambda b, hh, o: (b, 0, o + hh), o=off))
    cw = lambda off: pl.BlockSpec((3, HEAD_DIM), functools.partial(lambda b, hh, o: (0, o + hh), o=off))
    grow = lambda off: pl.BlockSpec((1, 1, nc, GDN_CHUNK),
                                    functools.partial(lambda b, hh, o: (b, o + hh, 0, 0), o=off))
    return pl.pallas_call(
        _gdn_kernel,
        grid=(bsz, h),
        in_specs=[col(0), col(h), col(2 * h), cw(0), cw(h), cw(2 * h),
                  pl.BlockSpec((1, seq, LANES), lambda b, hh: (b, 0, 0)),
                  grow(0), grow(h),
                  col(0),
                  pl.BlockSpec((1, HEAD_DIM), lambda b, hh: (0, 0))],
        out_specs=col(0),
        out_shape=jax.ShapeDtypeStruct((bsz, seq, GDN_WIDTH), _BF16),
        scratch_shapes=(
            [pltpu.VMEM((seq, HEAD_DIM), _F32)] * 7
            + [pltpu.VMEM((2 * seq, HEAD_DIM), _BF16)] * 2
            + [pltpu.VMEM((2 * seq, GDN_CHUNK), _BF16)] * 2
            + [pltpu.VMEM((seq, GDN_CHUNK), _BF16)] * 2
            + [pltpu.VMEM((nc, HEAD_DIM), _F32)] * 2),
        compiler_params=_params("arbitrary", "arbitrary"),
        name="gdn_mixer",
    )(qkv, qkv, qkv, conv_w, conv_w, conv_w, slab, slab_t, slab_t, z, norm_w.reshape(1, HEAD_DIM).astype(_F32))


ATT_RADIUS = 64
ATT_BQ = 128
ATT_WIDTH_KEYS = ATT_BQ + 2 * ATT_RADIUS
ATT_GROUP = 4
ATT_SPLIT = 4
assert all(w // (2 * d) == ATT_RADIUS for w, d in DILATED_PATTERNS)
assert tuple(d for _, d in DILATED_PATTERNS) == (1, ATT_SPLIT, ATT_SPLIT * ATT_SPLIT)


def _attn_blocks(qs, kws, vws, biases):
    scale = HEAD_DIM ** -0.5
    ss = [_dot_nt(q, kw) * scale + b for q, kw, b in zip(qs, kws, biases)]
    ms = [jnp.max(s, axis=-1, keepdims=True) for s in ss]
    es = [jnp.exp(s - m) for s, m in zip(ss, ms)]
    ls = [jnp.sum(e, axis=-1, keepdims=True) for e in es]
    accs = [_dot(e.astype(_BF16), vw) for e, vw in zip(es, vws)]
    return accs, ms, ls


def _attn_merge(acc_a, m_a, l_a, acc_b, m_b, l_b):
    m = jnp.maximum(m_a, m_b)
    wa = jnp.exp(m_a - m)
    wb = jnp.exp(m_b - m)
    return wa * acc_a + wb * acc_b, m, wa * l_a + wb * l_b


def _attn_kernel(slope_ref, q_ref, k_ref, v_ref, o_ref, nat, xq, xk, xv, acc4, m4, l4, acc1, m1, l1, bias_scr):
    seq = q_ref.shape[1]
    bq, width, radius, split, group = ATT_BQ, ATT_WIDTH_KEYS, ATT_RADIUS, ATT_SPLIT, ATT_GROUP
    n4 = seq // split
    n16 = n4 // split
    rep = lambda x: jnp.broadcast_to(x, (bq, HEAD_DIM))

    slope = slope_ref[0][:, 0:1]
    rel = (lax.broadcasted_iota(jnp.int32, (bq, width), 1) - lax.broadcasted_iota(jnp.int32, (bq, width), 0))
    for p, (_, dilation) in enumerate(DILATED_PATTERNS):
        for pos in range(3):
            dist = jnp.abs(rel - pos * radius)
            bias_scr[p, pos] = jnp.where(dist <= radius, -(slope * float(dilation)) * dist.astype(_F32), NEG_INF)

    def window(i0, n):
        ws = pl.multiple_of(jnp.clip(i0 - radius, 0, n - width), radius)
        return ws, (i0 - ws) // radius

    rows = 512
    for src, dst in ((q_ref, xq), (k_ref, xk), (v_ref, xv)):
        def widen(r, carry, src=src):
            sl = pl.ds(pl.multiple_of(r * rows, rows), rows)
            nat[sl, :] = src[0, sl, :].astype(_F32)
            return carry
        lax.fori_loop(0, seq // rows, widen, 0)
        for cls in range(split):
            def gather(r, carry, dst=dst, cls=cls):
                r0 = pl.multiple_of(r * rows, rows)
                dst[cls, pl.ds(r0, rows), :] = nat[pl.ds(cls + split * r0, rows, stride=split), :]
                return carry
            lax.fori_loop(0, n4 // rows, gather, 0)

    def dil4(i, carry):
        i0 = pl.multiple_of(i * bq, bq)
        ws, pos = window(i0, n4)
        bias = bias_scr[1, pos]
        qs = [xq[c, pl.ds(i0, bq), :].astype(_BF16) for c in range(split)]
        kws = [xk[c, pl.ds(ws, width), :].astype(_BF16) for c in range(split)]
        vws = [xv[c, pl.ds(ws, width), :].astype(_BF16) for c in range(split)]
        accs, ms, ls = _attn_blocks(qs, kws, vws, [bias] * split)
        for c in range(split):
            acc4[c, pl.ds(i0, bq), :] = accs[c]
            m4[c, pl.ds(i0, bq), :] = rep(ms[c])
            l4[c, pl.ds(i0, bq), :] = rep(ls[c])
        return carry

    lax.fori_loop(0, n4 // bq, dil4, 0)

    def dil16(i, carry):
        sub = i // (n16 // bq)
        i0 = pl.multiple_of((i % (n16 // bq)) * bq, bq)
        ws, pos = window(i0, n16)
        bias = bias_scr[2, pos]
        q_rows = pl.ds(sub + split * i0, bq, stride=split)
        k_rows = pl.ds(sub + split * ws, width, stride=split)
        qs = [xq[c, q_rows, :].astype(_BF16) for c in range(split)]
        kws = [xk[c, k_rows, :].astype(_BF16) for c in range(split)]
        vws = [xv[c, k_rows, :].astype(_BF16) for c in range(split)]
        accs, ms, ls = _attn_blocks(qs, kws, vws, [bias] * split)
        for c in range(split):
            acc, m, l = _attn_merge(acc4[c, q_rows, :], m4[c, q_rows, :], l4[c, q_rows, :], accs[c], ms[c], ls[c])
            acc4[c, q_rows, :] = acc
            m4[c, q_rows, :] = m
            l4[c, q_rows, :] = l
        return carry

    lax.fori_loop(0, split * (n16 // bq), dil16, 0)

    def dil1(i, carry):
        qs, kws, vws, biases, i0s = [], [], [], [], []
        for j in range(group):
            i0 = pl.multiple_of((i * group + j) * bq, bq)
            ws, pos = window(i0, seq)
            i0s.append(i0)
            qs.append(q_ref[0, pl.ds(i0, bq), :])
            kws.append(k_ref[0, pl.ds(ws, width), :])
            vws.append(v_ref[0, pl.ds(ws, width), :])
            biases.append(bias_scr[0, pos])
        accs, ms, ls = _attn_blocks(qs, kws, vws, biases)
        for j in range(group):
            acc1[pl.ds(i0s[j], bq), :] = accs[j]
            m1[pl.ds(i0s[j], bq), :] = rep(ms[j])
            l1[pl.ds(i0s[j], bq), :] = rep(ls[j])
        return carry

    lax.fori_loop(0, seq // (bq * group), dil1, 0)

    for cls in range(split):
        def finish(r, carry, cls=cls):
            r0 = pl.multiple_of(r * bq, bq)
            nat_rows = pl.ds(cls + split * r0, bq, stride=split)
            acc, _, l = _attn_merge(acc1[nat_rows, :], m1[nat_rows, :], l1[nat_rows, :],
                                    acc4[cls, pl.ds(r0, bq), :], m4[cls, pl.ds(r0, bq), :], l4[cls, pl.ds(r0, bq), :])
            nat[nat_rows, :] = acc / l
            return carry
        lax.fori_loop(0, n4 // bq, finish, 0)

    def narrow(r, carry):
        sl = pl.ds(pl.multiple_of(r * rows, rows), rows)
        o_ref[0, sl, :] = nat[sl, :].astype(o_ref.dtype)
        return carry

    lax.fori_loop(0, seq // rows, narrow, 0)


def _dilated_attention(qkv, slopes):
    bsz, seq, _ = qkv.shape
    h = ATT_HEADS
    n4 = seq // ATT_SPLIT
    n16 = n4 // ATT_SPLIT
    assert n16 % ATT_BQ == 0 and n16 >= ATT_WIDTH_KEYS and seq % (ATT_BQ * ATT_GROUP) == 0
    col = lambda off: pl.BlockSpec((1, seq, HEAD_DIM), functools.partial(lambda b, hh, o: (b, 0, o + hh), o=off))
    slab = pltpu.VMEM((ATT_SPLIT, n4, HEAD_DIM), _F32)
    nat = pltpu.VMEM((seq, HEAD_DIM), _F32)
    return pl.pallas_call(
        _attn_kernel,
        grid=(bsz, h),
        in_specs=[pl.BlockSpec((1, 1, LANES), lambda b, hh: (hh, 0, 0)), col(0), col(h), col(2 * h)],
        out_specs=col(0),
        out_shape=jax.ShapeDtypeStruct((bsz, seq, ATT_WIDTH), _BF16),
        scratch_shapes=[nat, slab, slab, slab, slab, slab, slab, nat, nat, nat,
                        pltpu.VMEM((len(DILATED_PATTERNS), 3, ATT_BQ, ATT_WIDTH_KEYS), _F32)],
        compiler_params=_params("arbitrary", "arbitrary"),
        name="dilated_attention",
    )(slopes, qkv, qkv, qkv)


def _ffn_up_kernel(a_ref, wg_ref, wu_ref, cg_ref, cu_ref, o_ref, wcat, *, tiles_per_seq, halo):
    i = pl.program_id(1)
    tn = wg_ref.shape[1]
    tm = o_ref.shape[0]

    @pl.when(i == 0)
    def _():
        _cast_rows(wg_ref, wcat.at[:, pl.ds(0, tn)])
        _cast_rows(wu_ref, wcat.at[:, pl.ds(tn, tn)])

    y_all = _dot(a_ref[...], wcat[...])
    y = y_all[halo:halo + tm]
    first = (i % tiles_per_seq) == 0
    last = (i % tiles_per_seq) == tiles_per_seq - 1
    prev_row = jnp.where(first, 0.0, y_all[halo - 1:halo])
    next_row = jnp.where(last, 0.0, y_all[halo + tm:halo + tm + 1])
    ym1, yp1 = _shift_rows(y, prev_row, next_row)
    cw = jnp.concatenate([cg_ref[...], cu_ref[...]], axis=1)
    conv = ym1 * cw[0:1, :] + y * cw[1:2, :] + yp1 * cw[2:3, :]
    gate = conv[:, :tn]
    up = conv[:, tn:]
    o_ref[...] = (gate * _sigmoid(gate) * up).astype(o_ref.dtype)


def _ffn_up(u_pad, w_up, conv_w, seq, pad_rows, tm=512, tn=256):
    m, k = u_pad.shape[0] - 2 * pad_rows, u_pad.shape[1]
    ffn = w_up.shape[1] // 2
    halo = BF16_ROWS
    assert ffn % tn == 0 and seq % tm == 0 and pad_rows >= halo
    nj = ffn // tn
    assert tm % halo == 0 and pad_rows % halo == 0
    a_spec = pl.BlockSpec((pl.Element(tm + 2 * halo), pl.Element(k)),
                          lambda j, i: ((i * (tm // halo) + (pad_rows // halo - 1)) * halo, 0))
    return pl.pallas_call(
        functools.partial(_ffn_up_kernel, tiles_per_seq=seq // tm, halo=halo),
        grid=(nj, m // tm),
        in_specs=[a_spec,
                  pl.BlockSpec((k, tn), lambda j, i: (0, j)),
                  pl.BlockSpec((k, tn), lambda j, i: (0, j + nj)),
                  pl.BlockSpec((3, tn), lambda j, i: (0, j)),
                  pl.BlockSpec((3, tn), lambda j, i: (0, j + nj))],
        out_specs=pl.BlockSpec((tm, tn), lambda j, i: (i, j)),
        out_shape=jax.ShapeDtypeStruct((m, ffn), _BF16),
        scratch_shapes=[pltpu.VMEM((k, 2 * tn), _BF16)],
        compiler_params=_params("arbitrary", "arbitrary"),
        name="ffn_up_conv_glu",
    )(u_pad, w_up, w_up, conv_w, conv_w)


def _ple_kernel(a_ref, wg_ref, p_ref, wp_ref, h_ref, o_ref, wgb, wpb):
    @pl.when(pl.program_id(1) == 0)
    def _():
        _cast_rows(wg_ref, wgb)
        _cast_rows(wp_ref, wpb)

    gate = _sigmoid(_dot(a_ref[...], wgb[...]))
    emb = _dot(p_ref[...], wpb[...])
    o_ref[...] = h_ref[...] + gate * emb


def _ple(u, w_gate, p, w_proj, h, tm=512, tn=512):
    m, k = u.shape
    n = w_gate.shape[1]
    kp = p.shape[1]
    return pl.pallas_call(
        _ple_kernel,
        grid=(n // tn, m // tm),
        in_specs=[pl.BlockSpec((tm, k), lambda j, i: (i, 0)),
                  pl.BlockSpec((k, tn), lambda j, i: (0, j)),
                  pl.BlockSpec((tm, kp), lambda j, i: (i, 0)),
                  pl.BlockSpec((kp, tn), lambda j, i: (0, j)),
                  pl.BlockSpec((tm, tn), lambda j, i: (i, j))],
        out_specs=pl.BlockSpec((tm, tn), lambda j, i: (i, j)),
        out_shape=jax.ShapeDtypeStruct((m, n), _F32),
        scratch_shapes=[pltpu.VMEM((k, tn), _BF16), pltpu.VMEM((kp, tn), _BF16)],
        compiler_params=_params("arbitrary", "arbitrary"),
        name="ple_gate",
    )(u, w_gate, p, w_proj, h)


def kernel(x, p, attn_norm, w_in, gdn_conv, gdn_a_log, gdn_dt_bias, gdn_out_norm, w_out, ffn_norm, w_up,
           ffn_conv, w_down, ple_norm, w_ple_gate, w_ple_proj, final_norm):
    bsz, seq, d_model = x.shape
    depth = w_in.shape[0]
    m = bsz * seq
    qkv_cols = 3 * GDN_WIDTH
    ab_off = qkv_cols + GDN_WIDTH
    att_off = ab_off + 4 * GDN_HEADS
    slopes = jnp.exp2(-ALIBI_MAX_BIAS * (jnp.arange(ATT_HEADS, dtype=_F32) + 1.0) / ATT_HEADS)
    slopes = jnp.broadcast_to(slopes[:, None, None], (ATT_HEADS, 1, LANES))

    h = x.reshape(m, d_model)
    for i in range(depth):
        u = _rmsnorm(h, attn_norm[i], _BF16)
        wi = w_in[i]
        gdn_qkv = _matmul([u], wi, n_cols=qkv_cols, tm=512, tn=512, out_dtype=_F32, name="proj_gdn_qkv")
        gdn_z = _matmul([u], wi, n_cols=GDN_WIDTH, col_off=qkv_cols, tm=512, tn=512, out_dtype=_BF16,
                        name="proj_gdn_z")
        ab = _matmul([u], wi, n_cols=LANES, col_off=ab_off, tm=512, tn=LANES, out_dtype=_F32, name="proj_gdn_ab")
        att_qkv = _matmul([u], wi[:, att_off:], n_cols=3 * ATT_WIDTH, tm=512, tn=512, out_dtype=_BF16,
                          name="proj_att_qkv")

        slab = _gdn_gates(ab, gdn_a_log[i], gdn_dt_bias[i]).reshape(bsz, seq, LANES)
        slab_t = slab.transpose(0, 2, 1).reshape(bsz, LANES, seq // GDN_CHUNK, GDN_CHUNK)
        o_gdn = _gdn_mixer(gdn_qkv.reshape(bsz, seq, qkv_cols), gdn_z.reshape(bsz, seq, GDN_WIDTH), slab, slab_t,
                           gdn_conv[i], gdn_out_norm[i]).reshape(m, GDN_WIDTH)

        att3 = att_qkv.reshape(bsz, seq, 3 * ATT_WIDTH)
        o_att = _dilated_attention(att3, slopes).reshape(m, ATT_WIDTH)

        h = _matmul([o_gdn, o_att], w_out[i], n_cols=d_model, tm=512, tn=512, out_dtype=_F32, residual=h,
                    name="out_proj")

        norm_tm = 256
        u_pad = _rmsnorm(h, ffn_norm[i], _BF16, tm=norm_tm, pad_tiles=1)
        act = _ffn_up(u_pad, w_up[i], ffn_conv[i], seq, pad_rows=norm_tm)
        h = _matmul([act], w_down[i].astype(_BF16), n_cols=d_model, tm=512, tn=512, out_dtype=_F32, residual=h,
                    name="ffn_down")

        u = _rmsnorm(h, ple_norm[i], _BF16)
        h = _ple(u, w_ple_gate[i], p[i].reshape(m, -1).astype(_BF16), w_ple_proj[i], h)
    return _rmsnorm(h, final_norm, x.dtype).reshape(bsz, seq, d_model)
```

```python
import functools

import jax
import jax.numpy as jnp
from jax import lax
from jax.experimental import pallas as pl
from jax.experimental.pallas import tpu as pltpu

HEAD_DIM = 128
GDN_HEADS = 16
ATT_HEADS = 16
GDN_WIDTH = GDN_HEADS * HEAD_DIM
ATT_WIDTH = ATT_HEADS * HEAD_DIM
GDN_CHUNK = 64
DILATED_PATTERNS = ((128, 1), (512, 4), (2048, 16))
ALIBI_MAX_BIAS = 8.0
NORM_EPS = 1e-6
NEG_INF = -1e30

LANES = 128
BF16_ROWS = 16
VMEM_LIMIT_BYTES = 56 * 1024 * 1024

_F32 = jnp.float32
_BF16 = jnp.bfloat16


def _params(*semantics):
    return pltpu.CompilerParams(dimension_semantics=semantics, vmem_limit_bytes=VMEM_LIMIT_BYTES)


def _sigmoid(x):
    return 1.0 / (1.0 + jnp.exp(-x))


def _dot(a, b):
    return jnp.dot(a, b, preferred_element_type=_F32)


def _dot_nt(a, b):
    return lax.dot_general(a, b, (((1,), (1,)), ((), ())), preferred_element_type=_F32)


def _rmsnorm_kernel(x_ref, w_ref, o_ref, *, pad_tiles):
    x = x_ref[...].astype(_F32)
    ms = jnp.mean(x * x, axis=-1, keepdims=True)
    y = x * lax.rsqrt(ms + NORM_EPS) * w_ref[...]
    if pad_tiles:
        i = pl.program_id(0)
        inside = jnp.logical_and(i >= pad_tiles, i < pl.num_programs(0) - pad_tiles)
        y = jnp.where(inside, y, 0.0)
    o_ref[...] = y.astype(o_ref.dtype)


def _rmsnorm(x, w, out_dtype, tm=256, pad_tiles=0):
    m, d = x.shape
    nt = m // tm
    return pl.pallas_call(
        functools.partial(_rmsnorm_kernel, pad_tiles=pad_tiles),
        grid=(nt + 2 * pad_tiles,),
        in_specs=[pl.BlockSpec((tm, d), lambda i: (jnp.clip(i - pad_tiles, 0, nt - 1), 0)),
                  pl.BlockSpec((1, d), lambda i: (0, 0))],
        out_specs=pl.BlockSpec((tm, d), lambda i: (i, 0)),
        out_shape=jax.ShapeDtypeStruct((m + 2 * pad_tiles * tm, d), out_dtype),
        compiler_params=_params("arbitrary"),
        name="rmsnorm",
    )(x, w.reshape(1, d).astype(_F32))


def _cast_rows(src_ref, dst_ref, rows=256):
    def body(r, carry):
        sl = pl.ds(pl.multiple_of(r * rows, rows), rows)
        dst_ref[sl, :] = src_ref[sl, :].astype(dst_ref.dtype)
        return carry
    lax.fori_loop(0, src_ref.shape[0] // rows, body, 0)


def _mm_kernel(*refs, n_a, has_res, cast_w):
    a_refs = refs[:n_a]
    w_refs = refs[n_a:2 * n_a]
    pos = 2 * n_a
    res_ref = refs[pos] if has_res else None
    pos += int(has_res)
    o_ref = refs[pos]
    wb_refs = refs[pos + 1:pos + 1 + n_a] if cast_w else w_refs
    if cast_w:
        @pl.when(pl.program_id(1) == 0)
        def _():
            for w_ref, wb_ref in zip(w_refs, wb_refs):
                _cast_rows(w_ref, wb_ref)
    acc = None
    for a_ref, wb_ref in zip(a_refs, wb_refs):
        part = _dot(a_ref[...], wb_ref[...])
        acc = part if acc is None else acc + part
    if has_res:
        acc = acc + res_ref[...]
    o_ref[...] = acc.astype(o_ref.dtype)


def _mm_shifted_kernel(a_ref, w_ref, wx_ref, o_ref, wb_ref, *, shift):
    tn = w_ref.shape[1]
    wide = tn + wx_ref.shape[1]

    @pl.when(pl.program_id(1) == 0)
    def _():
        rows = 256

        def body(r, carry):
            sl = pl.ds(pl.multiple_of(r * rows, rows), rows)
            both = jnp.concatenate([w_ref[sl, :], wx_ref[sl, :]], axis=1)
            wb_ref[sl, :] = pltpu.roll(both, wide - shift, 1)[:, :tn].astype(wb_ref.dtype)
            return carry

        lax.fori_loop(0, w_ref.shape[0] // rows, body, 0)

    o_ref[...] = _dot(a_ref[...], wb_ref[...]).astype(o_ref.dtype)


def _matmul_shifted(a, w, *, n_cols, col_off, tm, tn, out_dtype, name):
    m, k = a.shape
    shift = col_off % LANES
    base = col_off - shift
    assert base % tn == 0 and n_cols % tn == 0 and m % tm == 0 and tn % LANES == 0
    assert base + n_cols + shift <= w.shape[1]
    return pl.pallas_call(
        functools.partial(_mm_shifted_kernel, shift=shift),
        grid=(n_cols // tn, m // tm),
        in_specs=[pl.BlockSpec((tm, k), lambda j, i: (i, 0)),
                  pl.BlockSpec((k, tn), lambda j, i: (0, base // tn + j)),
                  pl.BlockSpec((k, LANES), lambda j, i: (0, (base + (j + 1) * tn) // LANES))],
        out_specs=pl.BlockSpec((tm, tn), lambda j, i: (i, j)),
        out_shape=jax.ShapeDtypeStruct((m, n_cols), out_dtype),
        scratch_shapes=[pltpu.VMEM((k, tn), _BF16)],
        compiler_params=_params("arbitrary", "arbitrary"),
        name=name,
    )(a, w, w)


def _matmul(a_list, w, *, n_cols, col_off=0, tm, tn, out_dtype, residual=None, name):
    m = a_list[0].shape[0]
    assert col_off % tn == 0 and n_cols % tn == 0 and m % tm == 0
    off = col_off // tn
    cast_w = w.dtype != _BF16
    in_specs, scratch = [], []
    for a in a_list:
        in_specs.append(pl.BlockSpec((tm, a.shape[1]), lambda j, i: (i, 0)))
    row_blk = 0
    for a in a_list:
        kk = a.shape[1]
        assert all(b.shape[1] == kk for b in a_list)
        in_specs.append(pl.BlockSpec((kk, tn), functools.partial(lambda j, i, rb: (rb, j + off), rb=row_blk)))
        if cast_w:
            scratch.append(pltpu.VMEM((kk, tn), _BF16))
        row_blk += 1
    args = list(a_list) + [w] * len(a_list)
    if residual is not None:
        in_specs.append(pl.BlockSpec((tm, tn), lambda j, i: (i, j)))
        args.append(residual)
    return pl.pallas_call(
        functools.partial(_mm_kernel, n_a=len(a_list), has_res=residual is not None, cast_w=cast_w),
        grid=(n_cols // tn, m // tm),
        in_specs=in_specs,
        out_specs=pl.BlockSpec((tm, tn), lambda j, i: (i, j)),
        out_shape=jax.ShapeDtypeStruct((m, n_cols), out_dtype),
        scratch_shapes=scratch,
        compiler_params=_params("arbitrary", "arbitrary"),
        name=name,
    )(*args)


def _gate_kernel(ab_ref, alog_ref, dtb_ref, o_ref):
    x = ab_ref[...]
    rows = x.shape[0]
    xa = x + dtb_ref[...]
    softplus = jnp.maximum(xa, 0.0) + jnp.log(1.0 + jnp.exp(-jnp.abs(xa)))
    g = -jnp.exp(alog_ref[...]) * softplus
    beta = _sigmoid(x)
    c = GDN_CHUNK
    ri = lax.broadcasted_iota(jnp.int32, (c, c), 0)
    ci = lax.broadcasted_iota(jnp.int32, (c, c), 1)
    lower = (ri >= ci).astype(_F32)
    upper = (ri <= ci).astype(_F32)
    lane = lax.broadcasted_iota(jnp.int32, (c, LANES), 1)
    for n in range(rows // c):
        gc = g[n * c:(n + 1) * c]
        prefix = jnp.dot(lower, gc, precision=lax.Precision.HIGHEST, preferred_element_type=_F32)
        suffix = jnp.dot(upper, gc, precision=lax.Precision.HIGHEST, preferred_element_type=_F32)
        out = jnp.where(lane < GDN_HEADS, prefix,
                        jnp.where(lane < 2 * GDN_HEADS, suffix,
                                  jnp.where(lane < 4 * GDN_HEADS, beta[n * c:(n + 1) * c], 0.0)))
        o_ref[n * c:(n + 1) * c, :] = out


def _gdn_gates(ab, a_log, dt_bias, tm=512):
    m = ab.shape[0]
    pad = LANES - 2 * GDN_HEADS
    alog_vec = jnp.pad(a_log.reshape(-1).astype(_F32), (0, pad)).reshape(1, LANES)
    dtb_vec = jnp.pad(dt_bias.reshape(-1).astype(_F32), (0, pad)).reshape(1, LANES)
    vec_spec = pl.BlockSpec((1, LANES), lambda i: (0, 0))
    return pl.pallas_call(
        _gate_kernel,
        grid=(m // tm,),
        in_specs=[pl.BlockSpec((tm, LANES), lambda i: (i, 0)), vec_spec, vec_spec],
        out_specs=pl.BlockSpec((tm, LANES), lambda i: (i, 0)),
        out_shape=jax.ShapeDtypeStruct((m, LANES), _F32),
        compiler_params=_params("arbitrary"),
        name="gdn_gates",
    )(ab, alog_vec, dtb_vec)


def _shift_rows(x, prev_row, next_row):
    rows = x.shape[0]
    row = lax.broadcasted_iota(jnp.int32, x.shape, 0)
    xm1 = jnp.where(row == 0, prev_row, pltpu.roll(x, 1, 0))
    xp1 = jnp.where(row == rows - 1, next_row, pltpu.roll(x, rows - 1, 0))
    return xm1, xp1


def _conv_silu(x_ref, w_ref, dst_ref, *, l2_scale, rows=256):
    seq = x_ref.shape[1]
    n_tiles = seq // rows
    w = w_ref[...]

    def body(r, carry):
        t0 = pl.multiple_of(r * rows, rows)
        x = x_ref[0, pl.ds(t0, rows), :]
        p0 = pl.multiple_of(jnp.maximum(t0 - 8, 0), 8)
        n0 = pl.multiple_of(jnp.minimum(t0 + rows, seq - 8), 8)
        prev_row = jnp.where(r == 0, 0.0, x_ref[0, pl.ds(p0, 8), :][7:8, :])
        next_row = jnp.where(r == n_tiles - 1, 0.0, x_ref[0, pl.ds(n0, 8), :][0:1, :])
        xm1, xp1 = _shift_rows(x, prev_row, next_row)
        y = xm1 * w[0:1, :] + x * w[1:2, :] + xp1 * w[2:3, :]
        y = y * _sigmoid(y)
        if l2_scale is not None:
            y = y * lax.rsqrt(jnp.sum(y * y, axis=-1, keepdims=True) + NORM_EPS) * l2_scale
        dst_ref[pl.ds(t0, rows), :] = y
        return carry

    lax.fori_loop(0, n_tiles, body, 0)


def _unit_tri_inverses(mats):
    c = mats[0].shape[0]
    ri = lax.broadcasted_iota(jnp.int32, (c, c), 0)
    ci = lax.broadcasted_iota(jnp.int32, (c, c), 1)
    eye = jnp.where(ri == ci, 1.0, 0.0)
    xs = [eye - a for a in mats]
    ps = [_dot(a.astype(_BF16), a.astype(_BF16)) for a in mats]
    power = 2
    while 2 * power < c:
        xps = [_dot(jnp.concatenate([x, p], axis=0).astype(_BF16), p.astype(_BF16)) for x, p in zip(xs, ps)]
        xs = [x + xp[:c] for x, xp in zip(xs, xps)]
        ps = [xp[c:] for xp in xps]
        power *= 2
    return [x + _dot(x.astype(_BF16), p.astype(_BF16)) for x, p in zip(xs, ps)]


def _gdn_prepare_chunks(cidxs, *, head, qn, kn, vn, slab_ref, grow_refs, dir_refs):
    c = GDN_CHUNK
    ri = lax.broadcasted_iota(jnp.int32, (c, c), 0)
    ci = lax.broadcasted_iota(jnp.int32, (c, c), 1)
    lane = lax.broadcasted_iota(jnp.int32, (c, LANES), 1)
    chunks = []
    for cidx in cidxs:
        r0 = pl.multiple_of(cidx * c, c)
        qc = qn[pl.ds(r0, c), :]
        kc = kn[pl.ds(r0, c), :]
        prod = _dot_nt(jnp.concatenate([qc, kc], axis=0).astype(_BF16), kc.astype(_BF16))
        chunks.append(dict(cidx=cidx, r0=r0, qc=qc, kc=kc, vc=vn[pl.ds(r0, c), :], qk=prod[:c], kk=prod[c:],
                           slab=slab_ref[0, pl.ds(r0, c), :]))
    chains = []
    for ch in chunks:
        for forward, grow_ref, refs in zip((True, False), grow_refs, dir_refs):
            g_lane = head if forward else GDN_HEADS + head
            b_lane = 2 * GDN_HEADS + g_lane
            gcol = jnp.sum(jnp.where(lane == g_lane, ch["slab"], 0.0), axis=-1, keepdims=True)
            bcol = jnp.sum(jnp.where(lane == b_lane, ch["slab"], 0.0), axis=-1, keepdims=True)
            grow = grow_ref[0, 0, pl.ds(ch["cidx"], 1), :]
            incl = (ri >= ci) if forward else (ri <= ci)
            strict = (ri > ci) if forward else (ri < ci)
            decay = jnp.where(incl, jnp.exp(jnp.where(incl, gcol - grow, 0.0)), 0.0)
            chains.append(dict(ch=ch, refs=refs, gcol=gcol, bcol=bcol, incl=incl, decay=decay,
                               glast=gcol[c - 1:c, :] if forward else gcol[0:1, :],
                               a=jnp.where(strict, ch["kk"] * decay * bcol, 0.0)))
    tinvs = _unit_tri_inverses([cn["a"] for cn in chains])
    uws = []
    for cn, tinv in zip(chains, tinvs):
        ch = cn["ch"]
        cn["eg"] = jnp.exp(cn["gcol"])
        rhs = jnp.concatenate([ch["vc"] * cn["bcol"], ch["kc"] * cn["bcol"] * cn["eg"]], axis=1).astype(_BF16)
        uws.append(_dot(tinv.astype(_BF16), rhs))
    for cn, uw in zip(chains, uws):
        ch = cn["ch"]
        u_ref, wq_ref, kdt_ref, intra_ref, dec_ref = cn["refs"]
        r0 = ch["r0"]
        r1 = pl.multiple_of(ch["cidx"] * 2 * c, 2 * c)
        u_ref[pl.ds(r0, c), :] = uw[:, :HEAD_DIM]
        wq_ref[pl.ds(r1, c), :] = uw[:, HEAD_DIM:].astype(_BF16)
        wq_ref[pl.ds(r1 + c, c), :] = (ch["qc"] * cn["eg"]).astype(_BF16)
        kdt_ref[pl.ds(r1, 2 * c), :] = (ch["kc"] * jnp.exp(cn["glast"] - cn["gcol"])).T.astype(_BF16)
        intra_ref[pl.ds(r0, c), :] = jnp.where(cn["incl"], ch["qk"] * cn["decay"], 0.0).astype(_BF16)
        dec_ref[pl.ds(ch["cidx"], 1), :] = jnp.broadcast_to(jnp.exp(cn["glast"]), (1, HEAD_DIM))


def _gdn_scan_chunks(cidxs, states, dir_refs, out_refs):
    c = GDN_CHUNK
    r0s = [pl.multiple_of(cidx * c, c) for cidx in cidxs]
    r1s = [pl.multiple_of(cidx * 2 * c, 2 * c) for cidx in cidxs]
    ws_qs = [_dot(refs[1][pl.ds(r1, 2 * c), :], s.astype(_BF16)) for refs, r1, s in zip(dir_refs, r1s, states)]
    v16s = [(refs[0][pl.ds(r0, c), :] - wq[:c]).astype(_BF16) for refs, r0, wq in zip(dir_refs, r0s, ws_qs)]
    new_states = [s * refs[4][pl.ds(cidx, 1), :] + _dot(refs[2][pl.ds(r1, 2 * c), :], v16)
                  for s, refs, cidx, r1, v16 in zip(states, dir_refs, cidxs, r1s, v16s)]
    for refs, out_ref, r0, wq, v16 in zip(dir_refs, out_refs, r0s, ws_qs, v16s):
        out_ref[pl.ds(r0, c), :] = wq[c:] + _dot(refs[3][pl.ds(r0, c), :], v16)
    return new_states


def _gdn_kernel(q_ref, k_ref, v_ref, cwq_ref, cwk_ref, cwv_ref, slab_ref, gf_ref, gb_ref, z_ref, nw_ref,
                o_ref, qn, kn, vn, of, ob, uf, ub, wqf, wqb, kdtf, kdtb, intraf, intrab, decf, decb):
    head = pl.program_id(1)
    seq = q_ref.shape[1]
    n_chunks = seq // GDN_CHUNK
    _conv_silu(q_ref, cwq_ref, qn, l2_scale=HEAD_DIM ** -0.5)
    _conv_silu(k_ref, cwk_ref, kn, l2_scale=1.0)
    _conv_silu(v_ref, cwv_ref, vn, l2_scale=None)

    fwd_refs = (uf, wqf, kdtf, intraf, decf)
    bwd_refs = (ub, wqb, kdtb, intrab, decb)
    group = 8

    def prepare(n, carry):
        _gdn_prepare_chunks([n * group + j for j in range(group)], head=head, qn=qn, kn=kn, vn=vn,
                            slab_ref=slab_ref, grow_refs=(gf_ref, gb_ref), dir_refs=(fwd_refs, bwd_refs))
        return carry

    lax.fori_loop(0, n_chunks // group, prepare, 0)

    def scan(n, carry):
        return tuple(_gdn_scan_chunks([n, n_chunks - 1 - n], carry, (fwd_refs, bwd_refs), (of, ob)))

    zero = jnp.zeros((HEAD_DIM, HEAD_DIM), _F32)
    lax.fori_loop(0, n_chunks, scan, (zero, zero))

    rows = 256

    def finish(r, carry):
        sl = pl.ds(pl.multiple_of(r * rows, rows), rows)
        o = of[sl, :] + ob[sl, :]
        o = o * lax.rsqrt(jnp.mean(o * o, axis=-1, keepdims=True) + NORM_EPS) * nw_ref[...]
        z = z_ref[0, sl, :].astype(_F32)
        o_ref[0, sl, :] = (o * (z * _sigmoid(z))).astype(o_ref.dtype)
        return carry

    lax.fori_loop(0, seq // rows, finish, 0)


def _gdn_mixer(qkv, z, slab, slab_t, conv_w, norm_w):
    bsz, seq, _ = qkv.shape
    nc = seq // GDN_CHUNK
    h = GDN_HEADS
    col = lambda off: pl.BlockSpec((1, seq, HEAD_DIM), functools.partial(lambda b, hh, o: (b, 0, o + hh), o=off))
    cw = lambda off: pl.BlockSpec((3, HEAD_DIM), functools.partial(lambda b, hh, o: (0, o + hh), o=off))
    grow = lambda off: pl.BlockSpec((1, 1, nc, GDN_CHUNK),
                                    functools.partial(lambda b, hh, o: (b, o + hh, 0, 0), o=off))
    return pl.pallas_call(
        _gdn_kernel,
        grid=(bsz, h),
        in_specs=[col(0), col(h), col(2 * h), cw(0), cw(h), cw(2 * h),
                  pl.BlockSpec((1, seq, LANES), lambda b, hh: (b, 0, 0)),
                  grow(0), grow(h),
                  col(0),
                  pl.BlockSpec((1, HEAD_DIM), lambda b, hh: (0, 0))],
        out_specs=col(0),
        out_shape=jax.ShapeDtypeStruct((bsz, seq, GDN_WIDTH), _BF16),
        scratch_shapes=(
            [pltpu.VMEM((seq, HEAD_DIM), _F32)] * 7
            + [pltpu.VMEM((2 * seq, HEAD_DIM), _BF16)] * 2
            + [pltpu.VMEM((2 * seq, GDN_CHUNK), _BF16)] * 2
            + [pltpu.VMEM((seq, GDN_CHUNK), _BF16)] * 2
            + [pltpu.VMEM((nc, HEAD_DIM), _F32)] * 2),
        compiler_params=_params("arbitrary", "arbitrary"),
        name="gdn_mixer",
    )(qkv, qkv, qkv, conv_w, conv_w, conv_w, slab, slab_t, slab_t, z, norm_w.reshape(1, HEAD_DIM).astype(_F32))


ATT_RADIUS = 64
ATT_BQ = 128
ATT_WIDTH_KEYS = ATT_BQ + 2 * ATT_RADIUS
ATT_GROUP = 4
ATT_SPLIT = 4
assert all(w // (2 * d) == ATT_RADIUS for w, d in DILATED_PATTERNS)
assert tuple(d for _, d in DILATED_PATTERNS) == (1, ATT_SPLIT, ATT_SPLIT * ATT_SPLIT)


def _attn_blocks(qs, kws, vws, biases):
    scale = HEAD_DIM ** -0.5
    ss = [_dot_nt(q, kw) * scale + b for q, kw, b in zip(qs, kws, biases)]
    ms = [jnp.max(s, axis=-1, keepdims=True) for s in ss]
    es = [jnp.exp(s - m) for s, m in zip(ss, ms)]
    ls = [jnp.sum(e, axis=-1, keepdims=True) for e in es]
    accs = [_dot(e.astype(_BF16), vw) for e, vw in zip(es, vws)]
    return accs, ms, ls


def _attn_merge(acc_a, m_a, l_a, acc_b, m_b, l_b):
    m = jnp.maximum(m_a, m_b)
    wa = jnp.exp(m_a - m)
    wb = jnp.exp(m_b - m)
    return wa * acc_a + wb * acc_b, m, wa * l_a + wb * l_b


def _attn_kernel(slope_ref, q_ref, k_ref, v_ref, o_ref, nat, xq, xk, xv, acc4, m4, l4, acc1, m1, l1, bias_scr):
    seq = q_ref.shape[1]
    bq, width, radius, split, group = ATT_BQ, ATT_WIDTH_KEYS, ATT_RADIUS, ATT_SPLIT, ATT_GROUP
    n4 = seq // split
    n16 = n4 // split
    rep = lambda x: jnp.broadcast_to(x, (bq, HEAD_DIM))

    slope = slope_ref[0][:, 0:1]
    rel = (lax.broadcasted_iota(jnp.int32, (bq, width), 1) - lax.broadcasted_iota(jnp.int32, (bq, width), 0))
    for p, (_, dilation) in enumerate(DILATED_PATTERNS):
        for pos in range(3):
            dist = jnp.abs(rel - pos * radius)
            bias_scr[p, pos] = jnp.where(dist <= radius, -(slope * float(dilation)) * dist.astype(_F32), NEG_INF)

    def window(i0, n):
        ws = pl.multiple_of(jnp.clip(i0 - radius, 0, n - width), radius)
        return ws, (i0 - ws) // radius

    rows = 512
    for src, dst in ((q_ref, xq), (k_ref, xk), (v_ref, xv)):
        def widen(r, carry, src=src):
            sl = pl.ds(pl.multiple_of(r * rows, rows), rows)
            nat[sl, :] = src[0, sl, :].astype(_F32)
            return carry
        lax.fori_loop(0, seq // rows, widen, 0)
        for cls in range(split):
            def gather(r, carry, dst=dst, cls=cls):
                r0 = pl.multiple_of(r * rows, rows)
                dst[cls, pl.ds(r0, rows), :] = nat[pl.ds(cls + split * r0, rows, stride=split), :]
                return carry
            lax.fori_loop(0, n4 // rows, gather, 0)

    def dil4(i, carry):
        i0 = pl.multiple_of(i * bq, bq)
        ws, pos = window(i0, n4)
        bias = bias_scr[1, pos]
        qs = [xq[c, pl.ds(i0, bq), :].astype(_BF16) for c in range(split)]
        kws = [xk[c, pl.ds(ws, width), :].astype(_BF16) for c in range(split)]
        vws = [xv[c, pl.ds(ws, width), :].astype(_BF16) for c in range(split)]
        accs, ms, ls = _attn_blocks(qs, kws, vws, [bias] * split)
        for c in range(split):
            acc4[c, pl.ds(i0, bq), :] = accs[c]
            m4[c, pl.ds(i0, bq), :] = rep(ms[c])
            l4[c, pl.ds(i0, bq), :] = rep(ls[c])
        return carry

    lax.fori_loop(0, n4 // bq, dil4, 0)

    def dil16(i, carry):
        sub = i // (n16 // bq)
        i0 = pl.multiple_of((i % (n16 // bq)) * bq, bq)
        ws, pos = window(i0, n16)
        bias = bias_scr[2, pos]
        q_rows = pl.ds(sub + split * i0, bq, stride=split)
        k_rows = pl.ds(sub + split * ws, width, stride=split)
        qs = [xq[c, q_rows, :].astype(_BF16) for c in range(split)]
        kws = [xk[c, k_rows, :].astype(_BF16) for c in range(split)]
        vws = [xv[c, k_rows, :].astype(_BF16) for c in range(split)]
        accs, ms, ls = _attn_blocks(qs, kws, vws, [bias] * split)
        for c in range(split):
            acc, m, l = _attn_merge(acc4[c, q_rows, :], m4[c, q_rows, :], l4[c, q_rows, :], accs[c], ms[c], ls[c])
            acc4[c, q_rows, :] = acc
            m4[c, q_rows, :] = m
            l4[c, q_rows, :] = l
        return carry

    lax.fori_loop(0, split * (n16 // bq), dil16, 0)

    def dil1(i, carry):
        qs, kws, vws, biases, i0s = [], [], [], [], []
        for j in range(group):
            i0 = pl.multiple_of((i * group + j) * bq, bq)
            ws, pos = window(i0, seq)
            i0s.append(i0)
            qs.append(q_ref[0, pl.ds(i0, bq), :])
            kws.append(k_ref[0, pl.ds(ws, width), :])
            vws.append(v_ref[0, pl.ds(ws, width), :])
            biases.append(bias_scr[0, pos])
        accs, ms, ls = _attn_blocks(qs, kws, vws, biases)
        for j in range(group):
            acc1[pl.ds(i0s[j], bq), :] = accs[j]
            m1[pl.ds(i0s[j], bq), :] = rep(ms[j])
            l1[pl.ds(i0s[j], bq), :] = rep(ls[j])
        return carry

    lax.fori_loop(0, seq // (bq * group), dil1, 0)

    for cls in range(split):
        def finish(r, carry, cls=cls):
            r0 = pl.multiple_of(r * bq, bq)
            nat_rows = pl.ds(cls + split * r0, bq, stride=split)
            acc, _, l = _attn_merge(acc1[nat_rows, :], m1[nat_rows, :], l1[nat_rows, :],
                                    acc4[cls, pl.ds(r0, bq), :], m4[cls, pl.ds(r0, bq), :], l4[cls, pl.ds(r0, bq), :])
            nat[nat_rows, :] = acc / l
            return carry
        lax.fori_loop(0, n4 // bq, finish, 0)

    def narrow(r, carry):
        sl = pl.ds(pl.multiple_of(r * rows, rows), rows)
        o_ref[0, sl, :] = nat[sl, :].astype(o_ref.dtype)
        return carry

    lax.fori_loop(0, seq // rows, narrow, 0)


def _dilated_attention(qkv, slopes):
    bsz, seq, _ = qkv.shape
    h = ATT_HEADS
    n4 = seq // ATT_SPLIT
    n16 = n4 // ATT_SPLIT
    assert n16 % ATT_BQ == 0 and n16 >= ATT_WIDTH_KEYS and seq % (ATT_BQ * ATT_GROUP) == 0
    col = lambda off: pl.BlockSpec((1, seq, HEAD_DIM), functools.partial(lambda b, hh, o: (b, 0, o + hh), o=off))
    slab = pltpu.VMEM((ATT_SPLIT, n4, HEAD_DIM), _F32)
    nat = pltpu.VMEM((seq, HEAD_DIM), _F32)
    return pl.pallas_call(
        _attn_kernel,
        grid=(bsz, h),
        in_specs=[pl.BlockSpec((1, 1, LANES), lambda b, hh: (hh, 0, 0)), col(0), col(h), col(2 * h)],
        out_specs=col(0),
        out_shape=jax.ShapeDtypeStruct((bsz, seq, ATT_WIDTH), _BF16),
        scratch_shapes=[nat, slab, slab, slab, slab, slab, slab, nat, nat, nat,
                        pltpu.VMEM((len(DILATED_PATTERNS), 3, ATT_BQ, ATT_WIDTH_KEYS), _F32)],
        compiler_params=_params("arbitrary", "arbitrary"),
        name="dilated_attention",
    )(slopes, qkv, qkv, qkv)


FFN_EPILOGUE_ROWS = 32

def _ffn_up_kernel(a_ref, wg_ref, wu_ref, cg_ref, cu_ref, o_ref, wcat, y_even, y_odd, *, n_tiles, tiles_per_seq, halo):
    s = pl.program_id(0)
    n_steps = pl.num_programs(0) - 1
    tn = wg_ref.shape[1]
    tm = o_ref.shape[0]

    @pl.when(jnp.logical_and(s % n_tiles == 0, s < n_steps))
    def _():
        _cast_rows(wg_ref, wcat.at[:, pl.ds(0, tn)])
        _cast_rows(wu_ref, wcat.at[:, pl.ds(tn, tn)])

    @pl.when(s == 0)
    def _():
        y_odd[...] = jnp.zeros(y_odd.shape, y_odd.dtype)

    tile = jnp.maximum(s - 1, 0) % n_tiles
    first = (tile % tiles_per_seq) == 0
    last = (tile % tiles_per_seq) == tiles_per_seq - 1

    def step(y_new, y_old):
        cw = jnp.concatenate([cg_ref[...], cu_ref[...]], axis=1)
        rows = FFN_EPILOGUE_ROWS
        proj = _dot(a_ref[...], wcat[...])
        for c in range(tm // rows):
            r0 = halo + c * rows
            y = y_old[pl.ds(r0, rows), :]
            prev_row = y_old[pl.ds(r0 - 8, 8), :][7:8, :]
            next_row = y_old[pl.ds(r0 + rows, 8), :][0:1, :]
            if c == 0:
                prev_row = jnp.where(first, 0.0, prev_row)
            if c == tm // rows - 1:
                next_row = jnp.where(last, 0.0, next_row)
            ym1, yp1 = _shift_rows(y, prev_row, next_row)
            conv = ym1 * cw[0:1, :] + y * cw[1:2, :] + yp1 * cw[2:3, :]
            gate = conv[:, :tn]
            up = conv[:, tn:]
            o_ref[pl.ds(c * rows, rows), :] = (gate * _sigmoid(gate) * up).astype(o_ref.dtype)
        y_new[...] = proj

    @pl.when(s % 2 == 0)
    def _():
        step(y_even, y_odd)

    @pl.when(s % 2 == 1)
    def _():
        step(y_odd, y_even)


def _ffn_up(u_pad, w_up, conv_w, seq, pad_rows, tm=1024, tn=256):
    m, k = u_pad.shape[0] - 2 * pad_rows, u_pad.shape[1]
    ffn = w_up.shape[1] // 2
    halo = BF16_ROWS
    assert ffn % tn == 0 and seq % tm == 0 and pad_rows >= halo
    nj = ffn // tn
    n_tiles = m // tm
    n_steps = nj * n_tiles
    assert tm % halo == 0 and pad_rows % halo == 0
    col = lambda s: jnp.minimum(s // n_tiles, nj - 1)
    col_prev = lambda s: jnp.maximum(s - 1, 0) // n_tiles
    a_spec = pl.BlockSpec((pl.Element(tm + 2 * halo), pl.Element(k)),
                          lambda s: (((s % n_tiles) * (tm // halo) + (pad_rows // halo - 1)) * halo, 0))
    return pl.pallas_call(
        functools.partial(_ffn_up_kernel, n_tiles=n_tiles, tiles_per_seq=seq // tm, halo=halo),
        grid=(n_steps + 1,),
        in_specs=[a_spec,
                  pl.BlockSpec((k, tn), lambda s: (0, col(s))),
                  pl.BlockSpec((k, tn), lambda s: (0, col(s) + nj)),
                  pl.BlockSpec((3, tn), lambda s: (0, col_prev(s))),
                  pl.BlockSpec((3, tn), lambda s: (0, col_prev(s) + nj))],
        out_specs=pl.BlockSpec((tm, tn), lambda s: (jnp.maximum(s - 1, 0) % n_tiles, col_prev(s))),
        out_shape=jax.ShapeDtypeStruct((m, ffn), _BF16),
        scratch_shapes=[pltpu.VMEM((k, 2 * tn), _BF16)] + [pltpu.VMEM((tm + 2 * halo, 2 * tn), _F32)] * 2,
        compiler_params=_params("arbitrary"),
        name="ffn_up_conv_glu",
    )(u_pad, w_up, w_up, conv_w, conv_w)


def _ple_kernel(a_ref, wg_ref, p_ref, wp_ref, h_ref, o_ref, wgb, wpb):
    @pl.when(pl.program_id(1) == 0)
    def _():
        _cast_rows(wg_ref, wgb)
        _cast_rows(wp_ref, wpb)

    gate = _sigmoid(_dot(a_ref[...], wgb[...]))
    emb = _dot(p_ref[...], wpb[...])
    o_ref[...] = h_ref[...] + gate * emb


def _ple(u, w_gate, p, w_proj, h, tm=512, tn=512):
    m, k = u.shape
    n = w_gate.shape[1]
    kp = p.shape[1]
    return pl.pallas_call(
        _ple_kernel,
        grid=(n // tn, m // tm),
        in_specs=[pl.BlockSpec((tm, k), lambda j, i: (i, 0)),
                  pl.BlockSpec((k, tn), lambda j, i: (0, j)),
                  pl.BlockSpec((tm, kp), lambda j, i: (i, 0)),
                  pl.BlockSpec((kp, tn), lambda j, i: (0, j)),
                  pl.BlockSpec((tm, tn), lambda j, i: (i, j))],
        out_specs=pl.BlockSpec((tm, tn), lambda j, i: (i, j)),
        out_shape=jax.ShapeDtypeStruct((m, n), _F32),
        scratch_shapes=[pltpu.VMEM((k, tn), _BF16), pltpu.VMEM((kp, tn), _BF16)],
        compiler_params=_params("arbitrary", "arbitrary"),
        name="ple_gate",
    )(u, w_gate, p, w_proj, h)


def kernel(x, p, attn_norm, w_in, gdn_conv, gdn_a_log, gdn_dt_bias, gdn_out_norm, w_out, ffn_norm, w_up,
           ffn_conv, w_down, ple_norm, w_ple_gate, w_ple_proj, final_norm):
    bsz, seq, d_model = x.shape
    depth = w_in.shape[0]
    m = bsz * seq
    qkv_cols = 3 * GDN_WIDTH
    ab_off = qkv_cols + GDN_WIDTH
    att_off = ab_off + 4 * GDN_HEADS
    slopes = jnp.exp2(-ALIBI_MAX_BIAS * (jnp.arange(ATT_HEADS, dtype=_F32) + 1.0) / ATT_HEADS)
    slopes = jnp.broadcast_to(slopes[:, None, None], (ATT_HEADS, 1, LANES))

    h = x.reshape(m, d_model)
    for i in range(depth):
        u = _rmsnorm(h, attn_norm[i], _BF16)
        wi = w_in[i]
        gdn_qkv = _matmul([u], wi, n_cols=qkv_cols, tm=512, tn=512, out_dtype=_F32, name="proj_gdn_qkv")
        gdn_z = _matmul([u], wi, n_cols=GDN_WIDTH, col_off=qkv_cols, tm=512, tn=512, out_dtype=_BF16,
                        name="proj_gdn_z")
        ab = _matmul([u], wi, n_cols=LANES, col_off=ab_off, tm=512, tn=LANES, out_dtype=_F32, name="proj_gdn_ab")
        att_qkv = _matmul_shifted(u, wi, n_cols=3 * ATT_WIDTH, col_off=att_off, tm=512, tn=512, out_dtype=_BF16,
                                  name="proj_att_qkv")

        slab = _gdn_gates(ab, gdn_a_log[i], gdn_dt_bias[i]).reshape(bsz, seq, LANES)
        slab_t = slab.transpose(0, 2, 1).reshape(bsz, LANES, seq // GDN_CHUNK, GDN_CHUNK)
        o_gdn = _gdn_mixer(gdn_qkv.reshape(bsz, seq, qkv_cols), gdn_z.reshape(bsz, seq, GDN_WIDTH), slab, slab_t,
                           gdn_conv[i], gdn_out_norm[i]).reshape(m, GDN_WIDTH)

        att3 = att_qkv.reshape(bsz, seq, 3 * ATT_WIDTH)
        o_att = _dilated_attention(att3, slopes).reshape(m, ATT_WIDTH)

        h = _matmul([o_gdn, o_att], w_out[i], n_cols=d_model, tm=512, tn=512, out_dtype=_F32, residual=h,
                    name="out_proj")

        norm_tm = 256
        u_pad = _rmsnorm(h, ffn_norm[i], _BF16, tm=norm_tm, pad_tiles=1)
        act = _ffn_up(u_pad, w_up[i], ffn_conv[i], seq, pad_rows=norm_tm)
        h = _matmul([act], w_down[i].astype(_BF16), n_cols=d_model, tm=512, tn=512, out_dtype=_F32, residual=h,
                    name="ffn_down")

        u = _rmsnorm(h, ple_norm[i], _BF16)
        h = _ple(u, w_ple_gate[i], p[i].reshape(m, -1).astype(_BF16), w_ple_proj[i], h)
    return _rmsnorm(h, final_norm, x.dtype).reshape(bsz, seq, d_model)
```

```python
import functools

import jax
import jax.numpy as jnp
from jax import lax
from jax.experimental import pallas as pl
from jax.experimental.pallas import tpu as pltpu

HEAD_DIM = 128
GDN_HEADS = 16
ATT_HEADS = 16
GDN_WIDTH = GDN_HEADS * HEAD_DIM
ATT_WIDTH = ATT_HEADS * HEAD_DIM
GDN_CHUNK = 64
DILATED_PATTERNS = ((128, 1), (512, 4), (2048, 16))
ALIBI_MAX_BIAS = 8.0
NORM_EPS = 1e-6
NEG_INF = -1e30

LANES = 128
SUBLANES = 8
BF16_ROWS = 16
VMEM_LIMIT_BYTES = 56 * 1024 * 1024
MM_TM = 1024
MM_TN = 512
MM_TM_LONG_K = 512

_F32 = jnp.float32
_BF16 = jnp.bfloat16


def _params(*semantics):
    return pltpu.CompilerParams(dimension_semantics=semantics, vmem_limit_bytes=VMEM_LIMIT_BYTES)


def _sigmoid(x):
    return 1.0 / (1.0 + jnp.exp(-x))


def _dot(a, b):
    return jnp.dot(a, b, preferred_element_type=_F32)


def _dot_nt(a, b):
    return lax.dot_general(a, b, (((1,), (1,)), ((), ())), preferred_element_type=_F32)


def _rmsnorm_kernel(x_ref, w_ref, o_ref, *, pad_tiles):
    x = x_ref[...].astype(_F32)
    ms = jnp.mean(x * x, axis=-1, keepdims=True)
    y = x * lax.rsqrt(ms + NORM_EPS) * w_ref[...]
    if pad_tiles:
        i = pl.program_id(0)
        inside = jnp.logical_and(i >= pad_tiles, i < pl.num_programs(0) - pad_tiles)
        y = jnp.where(inside, y, 0.0)
    o_ref[...] = y.astype(o_ref.dtype)


def _rmsnorm(x, w, out_dtype, tm=256, pad_tiles=0):
    m, d = x.shape
    nt = m // tm
    return pl.pallas_call(
        functools.partial(_rmsnorm_kernel, pad_tiles=pad_tiles),
        grid=(nt + 2 * pad_tiles,),
        in_specs=[pl.BlockSpec((tm, d), lambda i: (jnp.clip(i - pad_tiles, 0, nt - 1), 0)),
                  pl.BlockSpec((1, d), lambda i: (0, 0))],
        out_specs=pl.BlockSpec((tm, d), lambda i: (i, 0)),
        out_shape=jax.ShapeDtypeStruct((m + 2 * pad_tiles * tm, d), out_dtype),
        compiler_params=_params("arbitrary"),
        name="rmsnorm",
    )(x, w.reshape(1, d).astype(_F32))


def _cast_rows(src_ref, dst_ref, rows=256):
    def body(r, carry):
        sl = pl.ds(pl.multiple_of(r * rows, rows), rows)
        dst_ref[sl, :] = src_ref[sl, :].astype(dst_ref.dtype)
        return carry
    lax.fori_loop(0, src_ref.shape[0] // rows, body, 0)


def _mm_kernel(*refs, n_a, has_res, cast_w):
    a_refs = refs[:n_a]
    w_refs = refs[n_a:2 * n_a]
    pos = 2 * n_a
    res_ref = refs[pos] if has_res else None
    pos += int(has_res)
    o_ref = refs[pos]
    wb_refs = refs[pos + 1:pos + 1 + n_a] if cast_w else w_refs
    if cast_w:
        @pl.when(pl.program_id(1) == 0)
        def _():
            for w_ref, wb_ref in zip(w_refs, wb_refs):
                _cast_rows(w_ref, wb_ref)
    acc = None
    for a_ref, wb_ref in zip(a_refs, wb_refs):
        part = _dot(a_ref[...], wb_ref[...])
        acc = part if acc is None else acc + part
    if has_res:
        acc = acc + res_ref[...]
    o_ref[...] = acc.astype(o_ref.dtype)


def _mm_nt_kernel(a_ref, wt_ref, o_ref, wb_ref):
    @pl.when(pl.program_id(1) == 0)
    def _():
        _cast_rows(wt_ref, wb_ref, rows=SUBLANES * 8)

    o_ref[...] = _dot_nt(a_ref[...], wb_ref[...]).astype(o_ref.dtype)


def _matmul_nt(a, wt, *, n_cols, row_off, tm, tn, out_dtype, name):
    m, k = a.shape
    assert n_cols % tn == 0 and m % tm == 0 and row_off % SUBLANES == 0 and tn % SUBLANES == 0
    return pl.pallas_call(
        _mm_nt_kernel,
        grid=(n_cols // tn, m // tm),
        in_specs=[pl.BlockSpec((tm, k), lambda j, i: (i, 0)),
                  pl.BlockSpec((pl.Element(tn), pl.Element(k)),
                               lambda j, i: ((j * (tn // SUBLANES) + row_off // SUBLANES) * SUBLANES, 0))],
        out_specs=pl.BlockSpec((tm, tn), lambda j, i: (i, j)),
        out_shape=jax.ShapeDtypeStruct((m, n_cols), out_dtype),
        scratch_shapes=[pltpu.VMEM((tn, k), _BF16)],
        compiler_params=_params("arbitrary", "arbitrary"),
        name=name,
    )(a, wt)


def _matmul(a_list, w, *, n_cols, col_off=0, tm, tn, out_dtype, residual=None, name):
    m = a_list[0].shape[0]
    assert col_off % tn == 0 and n_cols % tn == 0 and m % tm == 0
    off = col_off // tn
    cast_w = w.dtype != _BF16
    in_specs, scratch = [], []
    for a in a_list:
        in_specs.append(pl.BlockSpec((tm, a.shape[1]), lambda j, i: (i, 0)))
    row_blk = 0
    for a in a_list:
        kk = a.shape[1]
        assert all(b.shape[1] == kk for b in a_list)
        in_specs.append(pl.BlockSpec((kk, tn), functools.partial(lambda j, i, rb: (rb, j + off), rb=row_blk)))
        if cast_w:
            scratch.append(pltpu.VMEM((kk, tn), _BF16))
        row_blk += 1
    args = list(a_list) + [w] * len(a_list)
    if residual is not None:
        in_specs.append(pl.BlockSpec((tm, tn), lambda j, i: (i, j)))
        args.append(residual)
    return pl.pallas_call(
        functools.partial(_mm_kernel, n_a=len(a_list), has_res=residual is not None, cast_w=cast_w),
        grid=(n_cols // tn, m // tm),
        in_specs=in_specs,
        out_specs=pl.BlockSpec((tm, tn), lambda j, i: (i, j)),
        out_shape=jax.ShapeDtypeStruct((m, n_cols), out_dtype),
        scratch_shapes=scratch,
        compiler_params=_params("arbitrary", "arbitrary"),
        name=name,
    )(*args)


def _gate_kernel(ab_ref, alog_ref, dtb_ref, o_ref):
    x = ab_ref[...]
    rows = x.shape[0]
    xa = x + dtb_ref[...]
    softplus = jnp.maximum(xa, 0.0) + jnp.log(1.0 + jnp.exp(-jnp.abs(xa)))
    g = -jnp.exp(alog_ref[...]) * softplus
    beta = _sigmoid(x)
    c = GDN_CHUNK
    ri = lax.broadcasted_iota(jnp.int32, (c, c), 0)
    ci = lax.broadcasted_iota(jnp.int32, (c, c), 1)
    lower = (ri >= ci).astype(_F32)
    upper = (ri <= ci).astype(_F32)
    lane = lax.broadcasted_iota(jnp.int32, (c, LANES), 1)
    for n in range(rows // c):
        gc = g[n * c:(n + 1) * c]
        prefix = jnp.dot(lower, gc, precision=lax.Precision.HIGHEST, preferred_element_type=_F32)
        suffix = jnp.dot(upper, gc, precision=lax.Precision.HIGHEST, preferred_element_type=_F32)
        out = jnp.where(lane < GDN_HEADS, prefix,
                        jnp.where(lane < 2 * GDN_HEADS, suffix,
                                  jnp.where(lane < 4 * GDN_HEADS, beta[n * c:(n + 1) * c], 0.0)))
        o_ref[n * c:(n + 1) * c, :] = out


def _gdn_gates(ab, a_log, dt_bias, tm=512):
    m = ab.shape[0]
    pad = LANES - 2 * GDN_HEADS
    alog_vec = jnp.pad(a_log.reshape(-1).astype(_F32), (0, pad)).reshape(1, LANES)
    dtb_vec = jnp.pad(dt_bias.reshape(-1).astype(_F32), (0, pad)).reshape(1, LANES)
    vec_spec = pl.BlockSpec((1, LANES), lambda i: (0, 0))
    return pl.pallas_call(
        _gate_kernel,
        grid=(m // tm,),
        in_specs=[pl.BlockSpec((tm, LANES), lambda i: (i, 0)), vec_spec, vec_spec],
        out_specs=pl.BlockSpec((tm, LANES), lambda i: (i, 0)),
        out_shape=jax.ShapeDtypeStruct((m, LANES), _F32),
        compiler_params=_params("arbitrary"),
        name="gdn_gates",
    )(ab, alog_vec, dtb_vec)


def _shift_rows(x, prev_row, next_row):
    rows = x.shape[0]
    row = lax.broadcasted_iota(jnp.int32, x.shape, 0)
    xm1 = jnp.where(row == 0, prev_row, pltpu.roll(x, 1, 0))
    xp1 = jnp.where(row == rows - 1, next_row, pltpu.roll(x, rows - 1, 0))
    return xm1, xp1


def _conv_silu(x_ref, w_ref, dst_ref, *, l2_scale, rows=256):
    seq = x_ref.shape[1]
    n_tiles = seq // rows
    w = w_ref[...]

    def body(r, carry):
        t0 = pl.multiple_of(r * rows, rows)
        x = x_ref[0, pl.ds(t0, rows), :]
        p0 = pl.multiple_of(jnp.maximum(t0 - 8, 0), 8)
        n0 = pl.multiple_of(jnp.minimum(t0 + rows, seq - 8), 8)
        prev_row = jnp.where(r == 0, 0.0, x_ref[0, pl.ds(p0, 8), :][7:8, :])
        next_row = jnp.where(r == n_tiles - 1, 0.0, x_ref[0, pl.ds(n0, 8), :][0:1, :])
        xm1, xp1 = _shift_rows(x, prev_row, next_row)
        y = xm1 * w[0:1, :] + x * w[1:2, :] + xp1 * w[2:3, :]
        y = y * _sigmoid(y)
        if l2_scale is not None:
            y = y * lax.rsqrt(jnp.sum(y * y, axis=-1, keepdims=True) + NORM_EPS) * l2_scale
        dst_ref[pl.ds(t0, rows), :] = y
        return carry

    lax.fori_loop(0, n_tiles, body, 0)


def _unit_tri_inverses(mats):
    c = mats[0].shape[0]
    ri = lax.broadcasted_iota(jnp.int32, (c, c), 0)
    ci = lax.broadcasted_iota(jnp.int32, (c, c), 1)
    eye = jnp.where(ri == ci, 1.0, 0.0)
    xs = [eye - a for a in mats]
    ps = [_dot(a.astype(_BF16), a.astype(_BF16)) for a in mats]
    power = 2
    while 2 * power < c:
        xps = [_dot(jnp.concatenate([x, p], axis=0).astype(_BF16), p.astype(_BF16)) for x, p in zip(xs, ps)]
        xs = [x + xp[:c] for x, xp in zip(xs, xps)]
        ps = [xp[c:] for xp in xps]
        power *= 2
    return [x + _dot(x.astype(_BF16), p.astype(_BF16)) for x, p in zip(xs, ps)]


GDN_GROUP = 8
GDN_RING_FIELDS = 5


def _gdn_ring_shapes():
    g, c, d = GDN_GROUP, GDN_CHUNK, HEAD_DIM
    return [pltpu.VMEM((g * 4 * c, d), _BF16), pltpu.VMEM((g * d, d), _BF16), pltpu.VMEM((g * c, d), _F32),
            pltpu.VMEM((g * c, c), _BF16), pltpu.VMEM((g, d), _F32)]


def _gdn_prepare_group(cidx_lists, *, head, qn, kn, vn, slab_ref, grow_refs, rings):
    c = GDN_CHUNK
    ri = lax.broadcasted_iota(jnp.int32, (c, c), 0)
    ci = lax.broadcasted_iota(jnp.int32, (c, c), 1)
    lane = lax.broadcasted_iota(jnp.int32, (c, LANES), 1)
    chains = []
    for forward, cidxs, grow_ref, ring in zip((True, False), cidx_lists, grow_refs, rings):
        for slot, cidx in enumerate(cidxs):
            r0 = pl.multiple_of(cidx * c, c)
            qc = qn[pl.ds(r0, c), :]
            kc = kn[pl.ds(r0, c), :]
            prod = _dot_nt(jnp.concatenate([qc, kc], axis=0).astype(_BF16), kc.astype(_BF16))
            chains.append(dict(forward=forward, slot=slot, cidx=cidx, ring=ring, grow_ref=grow_ref, qc=qc, kc=kc,
                               vc=vn[pl.ds(r0, c), :], qk=prod[:c], kk=prod[c:], slab=slab_ref[0, pl.ds(r0, c), :]))
    for cn in chains:
        forward = cn["forward"]
        g_lane = head if forward else GDN_HEADS + head
        b_lane = 2 * GDN_HEADS + g_lane
        gcol = jnp.sum(jnp.where(lane == g_lane, cn["slab"], 0.0), axis=-1, keepdims=True)
        bcol = jnp.sum(jnp.where(lane == b_lane, cn["slab"], 0.0), axis=-1, keepdims=True)
        grow = cn["grow_ref"][0, 0, pl.ds(cn["cidx"], 1), :]
        incl = (ri >= ci) if forward else (ri <= ci)
        strict = (ri > ci) if forward else (ri < ci)
        decay = jnp.where(incl, jnp.exp(jnp.where(incl, gcol - grow, 0.0)), 0.0)
        cn.update(gcol=gcol, bcol=bcol, incl=incl, decay=decay,
                  glast=gcol[c - 1:c, :] if forward else gcol[0:1, :],
                  a=jnp.where(strict, cn["kk"] * decay * bcol, 0.0))
    tinvs = _unit_tri_inverses([cn["a"] for cn in chains])
    uws = []
    for cn, tinv in zip(chains, tinvs):
        cn["eg"] = jnp.exp(cn["gcol"])
        rhs = jnp.concatenate([cn["vc"] * cn["bcol"], cn["kc"] * cn["bcol"] * cn["eg"]], axis=1).astype(_BF16)
        uws.append(_dot(tinv.astype(_BF16), rhs))
    mcs = []
    for cn, uw in zip(chains, uws):
        kdt = (cn["kc"] * jnp.exp(cn["glast"] - cn["gcol"])).T.astype(_BF16)
        mcs.append(_dot(kdt, uw.astype(_BF16)))
    for cn, uw, mc in zip(chains, uws, mcs):
        lhs_ref, c_ref, u_ref, intra_ref, dec_ref = cn["ring"]
        slot = cn["slot"]
        lhs_ref[pl.ds(slot * 4 * c, 2 * c), :] = (-mc[:, HEAD_DIM:]).astype(_BF16)
        lhs_ref[pl.ds(slot * 4 * c + 2 * c, c), :] = uw[:, HEAD_DIM:].astype(_BF16)
        lhs_ref[pl.ds(slot * 4 * c + 3 * c, c), :] = (cn["qc"] * cn["eg"]).astype(_BF16)
        c_ref[pl.ds(slot * 2 * c, 2 * c), :] = mc[:, :HEAD_DIM].astype(_BF16)
        u_ref[pl.ds(slot * c, c), :] = uw[:, :HEAD_DIM]
        intra_ref[pl.ds(slot * c, c), :] = jnp.where(cn["incl"], cn["qk"] * cn["decay"], 0.0).astype(_BF16)
        dec_ref[pl.ds(slot, 1), :] = jnp.broadcast_to(jnp.exp(cn["glast"]), (1, HEAD_DIM))


def _gdn_scan_group(states, cidx_lists, rings, out_refs):
    c = GDN_CHUNK
    for slot in range(GDN_GROUP):
        prods = [_dot(ring[0][pl.ds(slot * 4 * c, 4 * c), :], s.astype(_BF16)) for ring, s in zip(rings, states)]
        states = [s * ring[4][pl.ds(slot, 1), :] + p[:2 * c] + ring[1][pl.ds(slot * 2 * c, 2 * c), :].astype(_F32)
                  for ring, s, p in zip(rings, states, prods)]
        for ring, cidxs, out_ref, p in zip(rings, cidx_lists, out_refs, prods):
            v16 = (ring[2][pl.ds(slot * c, c), :] - p[2 * c:3 * c]).astype(_BF16)
            r0 = pl.multiple_of(cidxs[slot] * c, c)
            out_ref[pl.ds(r0, c), :] = p[3 * c:] + _dot(ring[3][pl.ds(slot * c, c), :], v16)
    return tuple(states)


def _gdn_kernel(q_ref, k_ref, v_ref, cwq_ref, cwk_ref, cwv_ref, slab_ref, gf_ref, gb_ref, z_ref, nw_ref,
                o_ref, qn, kn, vn, of, ob, *ring_refs):
    head = pl.program_id(1)
    seq = q_ref.shape[1]
    n_chunks = seq // GDN_CHUNK
    n_groups = n_chunks // GDN_GROUP
    _conv_silu(q_ref, cwq_ref, qn, l2_scale=HEAD_DIM ** -0.5)
    _conv_silu(k_ref, cwk_ref, kn, l2_scale=1.0)
    _conv_silu(v_ref, cwv_ref, vn, l2_scale=None)

    nf = GDN_RING_FIELDS
    sets = [[tuple(ring_refs[(2 * s + d) * nf:(2 * s + d + 1) * nf]) for d in range(2)] for s in range(2)]

    def groups(n):
        return ([n * GDN_GROUP + j for j in range(GDN_GROUP)],
                [n_chunks - 1 - n * GDN_GROUP - j for j in range(GDN_GROUP)])

    def prepare(n, rings):
        _gdn_prepare_group(groups(n), head=head, qn=qn, kn=kn, vn=vn, slab_ref=slab_ref,
                           grow_refs=(gf_ref, gb_ref), rings=rings)

    def scan(n, states, rings):
        return _gdn_scan_group(states, groups(n), rings, (of, ob))

    def stage_pair(m, states):
        n = 2 * m
        prepare(n + 1, sets[1])
        states = scan(n, states, sets[0])
        prepare(n + 2, sets[0])
        return scan(n + 1, states, sets[1])

    zero = jnp.zeros((HEAD_DIM, HEAD_DIM), _F32)
    prepare(0, sets[0])
    states = lax.fori_loop(0, n_groups // 2 - 1, stage_pair, (zero, zero))
    prepare(n_groups - 1, sets[1])
    states = scan(n_groups - 2, states, sets[0])
    scan(n_groups - 1, states, sets[1])

    rows = 256

    def finish(r, carry):
        sl = pl.ds(pl.multiple_of(r * rows, rows), rows)
        o = of[sl, :] + ob[sl, :]
        o = o * lax.rsqrt(jnp.mean(o * o, axis=-1, keepdims=True) + NORM_EPS) * nw_ref[...]
        z = z_ref[0, sl, :].astype(_F32)
        o_ref[0, sl, :] = (o * (z * _sigmoid(z))).astype(o_ref.dtype)
        return carry

    lax.fori_loop(0, seq // rows, finish, 0)


def _gdn_mixer(qkv, z, slab, slab_t, conv_w, norm_w):
    bsz, seq, _ = qkv.shape
    nc = seq // GDN_CHUNK
    h = GDN_HEADS
    col = lambda off: pl.BlockSpec((1, seq, HEAD_DIM), functools.partial(lambda b, hh, o: (b, 0, o + hh), o=off))
    cw = lambda off: pl.BlockSpec((3, HEAD_DIM), functools.partial(lambda b, hh, o: (0, o + hh), o=off))
    grow = lambda off: pl.BlockSpec((1, 1, nc, GDN_CHUNK),
                                    functools.partial(lambda b, hh, o: (b, o + hh, 0, 0), o=off))
    return pl.pallas_call(
        _gdn_kernel,
        grid=(bsz, h),
        in_specs=[col(0), col(h), col(2 * h), cw(0), cw(h), cw(2 * h),
                  pl.BlockSpec((1, seq, LANES), lambda b, hh: (b, 0, 0)),
                  grow(0), grow(h),
                  col(0),
                  pl.BlockSpec((1, HEAD_DIM), lambda b, hh: (0, 0))],
        out_specs=col(0),
        out_shape=jax.ShapeDtypeStruct((bsz, seq, GDN_WIDTH), _BF16),
        scratch_shapes=([pltpu.VMEM((seq, HEAD_DIM), _F32)] * 5
                        + _gdn_ring_shapes() * 4),
        compiler_params=_params("arbitrary", "arbitrary"),
        name="gdn_mixer",
    )(qkv, qkv, qkv, conv_w, conv_w, conv_w, slab, slab_t, slab_t, z, norm_w.reshape(1, HEAD_DIM).astype(_F32))


ATT_RADIUS = 64
ATT_BQ = 128
ATT_WIDTH_KEYS = ATT_BQ + 2 * ATT_RADIUS
ATT_GROUP = 4
ATT_SPLIT = 4
assert all(w // (2 * d) == ATT_RADIUS for w, d in DILATED_PATTERNS)
assert tuple(d for _, d in DILATED_PATTERNS) == (1, ATT_SPLIT, ATT_SPLIT * ATT_SPLIT)


def _attn_blocks(qs, kws, vws, biases):
    scale = HEAD_DIM ** -0.5
    ss = [_dot_nt(q, kw) * scale + b for q, kw, b in zip(qs, kws, biases)]
    ms = [jnp.max(s, axis=-1, keepdims=True) for s in ss]
    es = [jnp.exp(s - m) for s, m in zip(ss, ms)]
    ls = [jnp.sum(e, axis=-1, keepdims=True) for e in es]
    accs = [_dot(e.astype(_BF16), vw) for e, vw in zip(es, vws)]
    return accs, ms, ls


def _attn_merge(acc_a, m_a, l_a, acc_b, m_b, l_b):
    m = jnp.maximum(m_a, m_b)
    wa = jnp.exp(m_a - m)
    wb = jnp.exp(m_b - m)
    return wa * acc_a + wb * acc_b, m, wa * l_a + wb * l_b


def _attn_kernel(slope_ref, q_ref, k_ref, v_ref, o_ref, nat, xq, xk, xv, acc4, m4, l4, acc1, m1, l1, bias_scr):
    seq = q_ref.shape[1]
    bq, width, radius, split, group = ATT_BQ, ATT_WIDTH_KEYS, ATT_RADIUS, ATT_SPLIT, ATT_GROUP
    n4 = seq // split
    n16 = n4 // split
    rep = lambda x: jnp.broadcast_to(x, (bq, HEAD_DIM))

    slope = slope_ref[0][:, 0:1]
    rel = (lax.broadcasted_iota(jnp.int32, (bq, width), 1) - lax.broadcasted_iota(jnp.int32, (bq, width), 0))
    for p, (_, dilation) in enumerate(DILATED_PATTERNS):
        for pos in range(3):
            dist = jnp.abs(rel - pos * radius)
            bias_scr[p, pos] = jnp.where(dist <= radius, -(slope * float(dilation)) * dist.astype(_F32), NEG_INF)

    def window(i0, n):
        ws = pl.multiple_of(jnp.clip(i0 - radius, 0, n - width), radius)
        return ws, (i0 - ws) // radius

    rows = 512
    for src, dst in ((q_ref, xq), (k_ref, xk), (v_ref, xv)):
        def widen(r, carry, src=src):
            sl = pl.ds(pl.multiple_of(r * rows, rows), rows)
            nat[sl, :] = src[0, sl, :].astype(_F32)
            return carry
        lax.fori_loop(0, seq // rows, widen, 0)
        for cls in range(split):
            def gather(r, carry, dst=dst, cls=cls):
                r0 = pl.multiple_of(r * rows, rows)
                dst[cls, pl.ds(r0, rows), :] = nat[pl.ds(cls + split * r0, rows, stride=split), :]
                return carry
            lax.fori_loop(0, n4 // rows, gather, 0)

    def dil4(i, carry):
        i0 = pl.multiple_of(i * bq, bq)
        ws, pos = window(i0, n4)
        bias = bias_scr[1, pos]
        qs = [xq[c, pl.ds(i0, bq), :].astype(_BF16) for c in range(split)]
        kws = [xk[c, pl.ds(ws, width), :].astype(_BF16) for c in range(split)]
        vws = [xv[c, pl.ds(ws, width), :].astype(_BF16) for c in range(split)]
        accs, ms, ls = _attn_blocks(qs, kws, vws, [bias] * split)
        for c in range(split):
            acc4[c, pl.ds(i0, bq), :] = accs[c]
            m4[c, pl.ds(i0, bq), :] = rep(ms[c])
            l4[c, pl.ds(i0, bq), :] = rep(ls[c])
        return carry

    lax.fori_loop(0, n4 // bq, dil4, 0)

    def dil16(i, carry):
        sub = i // (n16 // bq)
        i0 = pl.multiple_of((i % (n16 // bq)) * bq, bq)
        ws, pos = window(i0, n16)
        bias = bias_scr[2, pos]
        q_rows = pl.ds(sub + split * i0, bq, stride=split)
        k_rows = pl.ds(sub + split * ws, width, stride=split)
        qs = [xq[c, q_rows, :].astype(_BF16) for c in range(split)]
        kws = [xk[c, k_rows, :].astype(_BF16) for c in range(split)]
        vws = [xv[c, k_rows, :].astype(_BF16) for c in range(split)]
        accs, ms, ls = _attn_blocks(qs, kws, vws, [bias] * split)
        for c in range(split):
            acc, m, l = _attn_merge(acc4[c, q_rows, :], m4[c, q_rows, :], l4[c, q_rows, :], accs[c], ms[c], ls[c])
            acc4[c, q_rows, :] = acc
            m4[c, q_rows, :] = m
            l4[c, q_rows, :] = l
        return carry

    lax.fori_loop(0, split * (n16 // bq), dil16, 0)

    def dil1(i, carry):
        qs, kws, vws, biases, i0s = [], [], [], [], []
        for j in range(group):
            i0 = pl.multiple_of((i * group + j) * bq, bq)
            ws, pos = window(i0, seq)
            i0s.append(i0)
            qs.append(q_ref[0, pl.ds(i0, bq), :])
            kws.append(k_ref[0, pl.ds(ws, width), :])
            vws.append(v_ref[0, pl.ds(ws, width), :])
            biases.append(bias_scr[0, pos])
        accs, ms, ls = _attn_blocks(qs, kws, vws, biases)
        for j in range(group):
            acc1[pl.ds(i0s[j], bq), :] = accs[j]
            m1[pl.ds(i0s[j], bq), :] = rep(ms[j])
            l1[pl.ds(i0s[j], bq), :] = rep(ls[j])
        return carry

    lax.fori_loop(0, seq // (bq * group), dil1, 0)

    for cls in range(split):
        def finish(r, carry, cls=cls):
            r0 = pl.multiple_of(r * bq, bq)
            nat_rows = pl.ds(cls + split * r0, bq, stride=split)
            acc, _, l = _attn_merge(acc1[nat_rows, :], m1[nat_rows, :], l1[nat_rows, :],
                                    acc4[cls, pl.ds(r0, bq), :], m4[cls, pl.ds(r0, bq), :], l4[cls, pl.ds(r0, bq), :])
            nat[nat_rows, :] = acc / l
            return carry
        lax.fori_loop(0, n4 // bq, finish, 0)

    def narrow(r, carry):
        sl = pl.ds(pl.multiple_of(r * rows, rows), rows)
        o_ref[0, sl, :] = nat[sl, :].astype(o_ref.dtype)
        return carry

    lax.fori_loop(0, seq // rows, narrow, 0)


def _dilated_attention(qkv, slopes):
    bsz, seq, _ = qkv.shape
    h = ATT_HEADS
    n4 = seq // ATT_SPLIT
    n16 = n4 // ATT_SPLIT
    assert n16 % ATT_BQ == 0 and n16 >= ATT_WIDTH_KEYS and seq % (ATT_BQ * ATT_GROUP) == 0
    col = lambda off: pl.BlockSpec((1, seq, HEAD_DIM), functools.partial(lambda b, hh, o: (b, 0, o + hh), o=off))
    slab = pltpu.VMEM((ATT_SPLIT, n4, HEAD_DIM), _F32)
    nat = pltpu.VMEM((seq, HEAD_DIM), _F32)
    return pl.pallas_call(
        _attn_kernel,
        grid=(bsz, h),
        in_specs=[pl.BlockSpec((1, 1, LANES), lambda b, hh: (hh, 0, 0)), col(0), col(h), col(2 * h)],
        out_specs=col(0),
        out_shape=jax.ShapeDtypeStruct((bsz, seq, ATT_WIDTH), _BF16),
        scratch_shapes=[nat, slab, slab, slab, slab, slab, slab, nat, nat, nat,
                        pltpu.VMEM((len(DILATED_PATTERNS), 3, ATT_BQ, ATT_WIDTH_KEYS), _F32)],
        compiler_params=_params("arbitrary", "arbitrary"),
        name="dilated_attention",
    )(slopes, qkv, qkv, qkv)


FFN_EPILOGUE_ROWS = 32

def _ffn_up_kernel(a_ref, wg_ref, wu_ref, cg_ref, cu_ref, o_ref, wcat, y_even, y_odd, *, n_tiles, tiles_per_seq, halo):
    s = pl.program_id(0)
    n_steps = pl.num_programs(0) - 1
    tn = wg_ref.shape[1]
    tm = o_ref.shape[0]

    @pl.when(jnp.logical_and(s % n_tiles == 0, s < n_steps))
    def _():
        _cast_rows(wg_ref, wcat.at[:, pl.ds(0, tn)])
        _cast_rows(wu_ref, wcat.at[:, pl.ds(tn, tn)])

    @pl.when(s == 0)
    def _():
        y_odd[...] = jnp.zeros(y_odd.shape, y_odd.dtype)

    tile = jnp.maximum(s - 1, 0) % n_tiles
    first = (tile % tiles_per_seq) == 0
    last = (tile % tiles_per_seq) == tiles_per_seq - 1

    def step(y_new, y_old):
        cw = jnp.concatenate([cg_ref[...], cu_ref[...]], axis=1)
        rows = FFN_EPILOGUE_ROWS
        proj = _dot(a_ref[...], wcat[...])
        for c in range(tm // rows):
            r0 = halo + c * rows
            y = y_old[pl.ds(r0, rows), :]
            prev_row = y_old[pl.ds(r0 - 8, 8), :][7:8, :]
            next_row = y_old[pl.ds(r0 + rows, 8), :][0:1, :]
            if c == 0:
                prev_row = jnp.where(first, 0.0, prev_row)
            if c == tm // rows - 1:
                next_row = jnp.where(last, 0.0, next_row)
            ym1, yp1 = _shift_rows(y, prev_row, next_row)
            conv = ym1 * cw[0:1, :] + y * cw[1:2, :] + yp1 * cw[2:3, :]
            gate = conv[:, :tn]
            up = conv[:, tn:]
            o_ref[pl.ds(c * rows, rows), :] = (gate * _sigmoid(gate) * up).astype(o_ref.dtype)
        y_new[...] = proj

    @pl.when(s % 2 == 0)
    def _():
        step(y_even, y_odd)

    @pl.when(s % 2 == 1)
    def _():
        step(y_odd, y_even)


def _ffn_up(u_pad, w_up, conv_w, seq, pad_rows, tm=1024, tn=256):
    m, k = u_pad.shape[0] - 2 * pad_rows, u_pad.shape[1]
    ffn = w_up.shape[1] // 2
    halo = BF16_ROWS
    assert ffn % tn == 0 and seq % tm == 0 and pad_rows >= halo
    nj = ffn // tn
    n_tiles = m // tm
    n_steps = nj * n_tiles
    assert tm % halo == 0 and pad_rows % halo == 0
    col = lambda s: jnp.minimum(s // n_tiles, nj - 1)
    col_prev = lambda s: jnp.maximum(s - 1, 0) // n_tiles
    a_spec = pl.BlockSpec((pl.Element(tm + 2 * halo), pl.Element(k)),
                          lambda s: (((s % n_tiles) * (tm // halo) + (pad_rows // halo - 1)) * halo, 0))
    return pl.pallas_call(
        functools.partial(_ffn_up_kernel, n_tiles=n_tiles, tiles_per_seq=seq // tm, halo=halo),
        grid=(n_steps + 1,),
        in_specs=[a_spec,
                  pl.BlockSpec((k, tn), lambda s: (0, col(s))),
                  pl.BlockSpec((k, tn), lambda s: (0, col(s) + nj)),
                  pl.BlockSpec((3, tn), lambda s: (0, col_prev(s))),
                  pl.BlockSpec((3, tn), lambda s: (0, col_prev(s) + nj))],
        out_specs=pl.BlockSpec((tm, tn), lambda s: (jnp.maximum(s - 1, 0) % n_tiles, col_prev(s))),
        out_shape=jax.ShapeDtypeStruct((m, ffn), _BF16),
        scratch_shapes=[pltpu.VMEM((k, 2 * tn), _BF16)] + [pltpu.VMEM((tm + 2 * halo, 2 * tn), _F32)] * 2,
        compiler_params=_params("arbitrary"),
        name="ffn_up_conv_glu",
    )(u_pad, w_up, w_up, conv_w, conv_w)


def _ple_kernel(a_ref, wg_ref, p_ref, wp_ref, h_ref, o_ref, wgb, wpb):
    @pl.when(pl.program_id(1) == 0)
    def _():
        _cast_rows(wg_ref, wgb)
        _cast_rows(wp_ref, wpb)

    gate = _sigmoid(_dot(a_ref[...], wgb[...]))
    emb = _dot(p_ref[...], wpb[...])
    o_ref[...] = h_ref[...] + gate * emb


def _ple(u, w_gate, p, w_proj, h, *, tm, tn):
    m, k = u.shape
    n = w_gate.shape[1]
    kp = p.shape[1]
    return pl.pallas_call(
        _ple_kernel,
        grid=(n // tn, m // tm),
        in_specs=[pl.BlockSpec((tm, k), lambda j, i: (i, 0)),
                  pl.BlockSpec((k, tn), lambda j, i: (0, j)),
                  pl.BlockSpec((tm, kp), lambda j, i: (i, 0)),
                  pl.BlockSpec((kp, tn), lambda j, i: (0, j)),
                  pl.BlockSpec((tm, tn), lambda j, i: (i, j))],
        out_specs=pl.BlockSpec((tm, tn), lambda j, i: (i, j)),
        out_shape=jax.ShapeDtypeStruct((m, n), _F32),
        scratch_shapes=[pltpu.VMEM((k, tn), _BF16), pltpu.VMEM((kp, tn), _BF16)],
        compiler_params=_params("arbitrary", "arbitrary"),
        name="ple_gate",
    )(u, w_gate, p, w_proj, h)


def kernel(x, p, attn_norm, w_in, gdn_conv, gdn_a_log, gdn_dt_bias, gdn_out_norm, w_out, ffn_norm, w_up,
           ffn_conv, w_down, ple_norm, w_ple_gate, w_ple_proj, final_norm):
    bsz, seq, d_model = x.shape
    depth = w_in.shape[0]
    m = bsz * seq
    qkv_cols = 3 * GDN_WIDTH
    ab_off = qkv_cols + GDN_WIDTH
    att_off = ab_off + 4 * GDN_HEADS
    slopes = jnp.exp2(-ALIBI_MAX_BIAS * (jnp.arange(ATT_HEADS, dtype=_F32) + 1.0) / ATT_HEADS)
    slopes = jnp.broadcast_to(slopes[:, None, None], (ATT_HEADS, 1, LANES))

    h = x.reshape(m, d_model)
    for i in range(depth):
        u = _rmsnorm(h, attn_norm[i], _BF16)
        wt = jnp.swapaxes(w_in[i], 0, 1)
        tm, tn = MM_TM, MM_TN
        gdn_qkv = _matmul_nt(u, wt, n_cols=qkv_cols, row_off=0, tm=tm, tn=tn, out_dtype=_F32, name="proj_gdn_qkv")
        gdn_z = _matmul_nt(u, wt, n_cols=GDN_WIDTH, row_off=qkv_cols, tm=tm, tn=tn, out_dtype=_BF16,
                           name="proj_gdn_z")
        ab = _matmul_nt(u, wt, n_cols=LANES, row_off=ab_off, tm=tm, tn=LANES, out_dtype=_F32, name="proj_gdn_ab")
        att_qkv = _matmul_nt(u, wt, n_cols=3 * ATT_WIDTH, row_off=att_off, tm=tm, tn=tn, out_dtype=_BF16,
                             name="proj_att_qkv")

        slab = _gdn_gates(ab, gdn_a_log[i], gdn_dt_bias[i]).reshape(bsz, seq, LANES)
        slab_t = slab.transpose(0, 2, 1).reshape(bsz, LANES, seq // GDN_CHUNK, GDN_CHUNK)
        o_gdn = _gdn_mixer(gdn_qkv.reshape(bsz, seq, qkv_cols), gdn_z.reshape(bsz, seq, GDN_WIDTH), slab, slab_t,
                           gdn_conv[i], gdn_out_norm[i]).reshape(m, GDN_WIDTH)

        att3 = att_qkv.reshape(bsz, seq, 3 * ATT_WIDTH)
        o_att = _dilated_attention(att3, slopes).reshape(m, ATT_WIDTH)

        h = _matmul([o_gdn, o_att], w_out[i], n_cols=d_model, tm=tm, tn=tn, out_dtype=_F32, residual=h,
                    name="out_proj")

        norm_tm = 256
        u_pad = _rmsnorm(h, ffn_norm[i], _BF16, tm=norm_tm, pad_tiles=1)
        act = _ffn_up(u_pad, w_up[i], ffn_conv[i], seq, pad_rows=norm_tm)
        h = _matmul([act], w_down[i].astype(_BF16), n_cols=d_model, tm=MM_TM_LONG_K, tn=tn, out_dtype=_F32,
                    residual=h, name="ffn_down")

        u = _rmsnorm(h, ple_norm[i], _BF16)
        h = _ple(u, w_ple_gate[i], p[i].reshape(m, -1).astype(_BF16), w_ple_proj[i], h, tm=tm, tn=tn)
    return _rmsnorm(h, final_norm, x.dtype).reshape(bsz, seq, d_model)
```

```python
import functools

import jax
import jax.numpy as jnp
from jax import lax
from jax.experimental import pallas as pl
from jax.experimental.pallas import tpu as pltpu

HEAD_DIM = 128
GDN_HEADS = 16
ATT_HEADS = 16
GDN_WIDTH = GDN_HEADS * HEAD_DIM
ATT_WIDTH = ATT_HEADS * HEAD_DIM
GDN_CHUNK = 64
DILATED_PATTERNS = ((128, 1), (512, 4), (2048, 16))
ALIBI_MAX_BIAS = 8.0
NORM_EPS = 1e-6
NEG_INF = -1e30

LANES = 128
SUBLANES = 8
BF16_ROWS = 16
VMEM_LIMIT_BYTES = 56 * 1024 * 1024
MM_TM = 1024
MM_TN = 512
MM_TM_LONG_K = 512

_F32 = jnp.float32
_BF16 = jnp.bfloat16


def _params(*semantics):
    return pltpu.CompilerParams(dimension_semantics=semantics, vmem_limit_bytes=VMEM_LIMIT_BYTES)


def _sigmoid(x):
    return 1.0 / (1.0 + jnp.exp(-x))


def _dot(a, b):
    return jnp.dot(a, b, preferred_element_type=_F32)


def _dot_nt(a, b):
    return lax.dot_general(a, b, (((1,), (1,)), ((), ())), preferred_element_type=_F32)


def _rmsnorm_kernel(x_ref, w_ref, o_ref, *, pad_tiles):
    x = x_ref[...].astype(_F32)
    ms = jnp.mean(x * x, axis=-1, keepdims=True)
    y = x * lax.rsqrt(ms + NORM_EPS) * w_ref[...]
    if pad_tiles:
        i = pl.program_id(0)
        inside = jnp.logical_and(i >= pad_tiles, i < pl.num_programs(0) - pad_tiles)
        y = jnp.where(inside, y, 0.0)
    o_ref[...] = y.astype(o_ref.dtype)


def _rmsnorm(x, w, out_dtype, tm=256, pad_tiles=0):
    m, d = x.shape
    nt = m // tm
    return pl.pallas_call(
        functools.partial(_rmsnorm_kernel, pad_tiles=pad_tiles),
        grid=(nt + 2 * pad_tiles,),
        in_specs=[pl.BlockSpec((tm, d), lambda i: (jnp.clip(i - pad_tiles, 0, nt - 1), 0)),
                  pl.BlockSpec((1, d), lambda i: (0, 0))],
        out_specs=pl.BlockSpec((tm, d), lambda i: (i, 0)),
        out_shape=jax.ShapeDtypeStruct((m + 2 * pad_tiles * tm, d), out_dtype),
        compiler_params=_params("arbitrary"),
        name="rmsnorm",
    )(x, w.reshape(1, d).astype(_F32))


def _cast_rows(src_ref, dst_ref, rows=256):
    def body(r, carry):
        sl = pl.ds(pl.multiple_of(r * rows, rows), rows)
        dst_ref[sl, :] = src_ref[sl, :].astype(dst_ref.dtype)
        return carry
    lax.fori_loop(0, src_ref.shape[0] // rows, body, 0)


def _mm_kernel(*refs, n_a, has_res, cast_w):
    a_refs = refs[:n_a]
    w_refs = refs[n_a:2 * n_a]
    pos = 2 * n_a
    res_ref = refs[pos] if has_res else None
    pos += int(has_res)
    o_ref = refs[pos]
    wb_refs = refs[pos + 1:pos + 1 + n_a] if cast_w else w_refs
    if cast_w:
        @pl.when(pl.program_id(1) == 0)
        def _():
            for w_ref, wb_ref in zip(w_refs, wb_refs):
                _cast_rows(w_ref, wb_ref)
    acc = None
    for a_ref, wb_ref in zip(a_refs, wb_refs):
        part = _dot(a_ref[...], wb_ref[...])
        acc = part if acc is None else acc + part
    if has_res:
        acc = acc + res_ref[...]
    o_ref[...] = acc.astype(o_ref.dtype)


def _mm_nt_kernel(a_ref, wt_ref, o_ref, wb_ref):
    @pl.when(pl.program_id(1) == 0)
    def _():
        _cast_rows(wt_ref, wb_ref, rows=SUBLANES * 8)

    o_ref[...] = _dot_nt(a_ref[...], wb_ref[...]).astype(o_ref.dtype)


def _matmul_nt(a, wt, *, n_cols, row_off, tm, tn, out_dtype, name):
    m, k = a.shape
    assert n_cols % tn == 0 and m % tm == 0 and row_off % SUBLANES == 0 and tn % SUBLANES == 0
    return pl.pallas_call(
        _mm_nt_kernel,
        grid=(n_cols // tn, m // tm),
        in_specs=[pl.BlockSpec((tm, k), lambda j, i: (i, 0)),
                  pl.BlockSpec((pl.Element(tn), pl.Element(k)),
                               lambda j, i: ((j * (tn // SUBLANES) + row_off // SUBLANES) * SUBLANES, 0))],
        out_specs=pl.BlockSpec((tm, tn), lambda j, i: (i, j)),
        out_shape=jax.ShapeDtypeStruct((m, n_cols), out_dtype),
        scratch_shapes=[pltpu.VMEM((tn, k), _BF16)],
        compiler_params=_params("arbitrary", "arbitrary"),
        name=name,
    )(a, wt)


def _matmul(a_list, w, *, n_cols, col_off=0, tm, tn, out_dtype, residual=None, name):
    m = a_list[0].shape[0]
    assert col_off % tn == 0 and n_cols % tn == 0 and m % tm == 0
    off = col_off // tn
    cast_w = w.dtype != _BF16
    in_specs, scratch = [], []
    for a in a_list:
        in_specs.append(pl.BlockSpec((tm, a.shape[1]), lambda j, i: (i, 0)))
    row_blk = 0
    for a in a_list:
        kk = a.shape[1]
        assert all(b.shape[1] == kk for b in a_list)
        in_specs.append(pl.BlockSpec((kk, tn), functools.partial(lambda j, i, rb: (rb, j + off), rb=row_blk)))
        if cast_w:
            scratch.append(pltpu.VMEM((kk, tn), _BF16))
        row_blk += 1
    args = list(a_list) + [w] * len(a_list)
    if residual is not None:
        in_specs.append(pl.BlockSpec((tm, tn), lambda j, i: (i, j)))
        args.append(residual)
    return pl.pallas_call(
        functools.partial(_mm_kernel, n_a=len(a_list), has_res=residual is not None, cast_w=cast_w),
        grid=(n_cols // tn, m // tm),
        in_specs=in_specs,
        out_specs=pl.BlockSpec((tm, tn), lambda j, i: (i, j)),
        out_shape=jax.ShapeDtypeStruct((m, n_cols), out_dtype),
        scratch_shapes=scratch,
        compiler_params=_params("arbitrary", "arbitrary"),
        name=name,
    )(*args)


def _gate_kernel(ab_ref, alog_ref, dtb_ref, o_ref):
    x = ab_ref[...]
    rows = x.shape[0]
    xa = x + dtb_ref[...]
    softplus = jnp.maximum(xa, 0.0) + jnp.log(1.0 + jnp.exp(-jnp.abs(xa)))
    g = -jnp.exp(alog_ref[...]) * softplus
    beta = _sigmoid(x)
    c = GDN_CHUNK
    ri = lax.broadcasted_iota(jnp.int32, (c, c), 0)
    ci = lax.broadcasted_iota(jnp.int32, (c, c), 1)
    lower = (ri >= ci).astype(_F32)
    upper = (ri <= ci).astype(_F32)
    lane = lax.broadcasted_iota(jnp.int32, (c, LANES), 1)
    for n in range(rows // c):
        gc = g[n * c:(n + 1) * c]
        prefix = jnp.dot(lower, gc, precision=lax.Precision.HIGHEST, preferred_element_type=_F32)
        suffix = jnp.dot(upper, gc, precision=lax.Precision.HIGHEST, preferred_element_type=_F32)
        out = jnp.where(lane < GDN_HEADS, prefix,
                        jnp.where(lane < 2 * GDN_HEADS, suffix,
                                  jnp.where(lane < 4 * GDN_HEADS, beta[n * c:(n + 1) * c], 0.0)))
        o_ref[n * c:(n + 1) * c, :] = out


def _gdn_gates(ab, a_log, dt_bias, tm=512):
    m = ab.shape[0]
    pad = LANES - 2 * GDN_HEADS
    alog_vec = jnp.pad(a_log.reshape(-1).astype(_F32), (0, pad)).reshape(1, LANES)
    dtb_vec = jnp.pad(dt_bias.reshape(-1).astype(_F32), (0, pad)).reshape(1, LANES)
    vec_spec = pl.BlockSpec((1, LANES), lambda i: (0, 0))
    return pl.pallas_call(
        _gate_kernel,
        grid=(m // tm,),
        in_specs=[pl.BlockSpec((tm, LANES), lambda i: (i, 0)), vec_spec, vec_spec],
        out_specs=pl.BlockSpec((tm, LANES), lambda i: (i, 0)),
        out_shape=jax.ShapeDtypeStruct((m, LANES), _F32),
        compiler_params=_params("arbitrary"),
        name="gdn_gates",
    )(ab, alog_vec, dtb_vec)


def _shift_rows(x, prev_row, next_row):
    rows = x.shape[0]
    row = lax.broadcasted_iota(jnp.int32, x.shape, 0)
    xm1 = jnp.where(row == 0, prev_row, pltpu.roll(x, 1, 0))
    xp1 = jnp.where(row == rows - 1, next_row, pltpu.roll(x, rows - 1, 0))
    return xm1, xp1


def _conv_silu(x_ref, w_ref, dst_ref, *, l2_scale, rows=256):
    seq = x_ref.shape[1]
    n_tiles = seq // rows
    w = w_ref[...]

    def body(r, carry):
        t0 = pl.multiple_of(r * rows, rows)
        x = x_ref[0, pl.ds(t0, rows), :]
        p0 = pl.multiple_of(jnp.maximum(t0 - 8, 0), 8)
        n0 = pl.multiple_of(jnp.minimum(t0 + rows, seq - 8), 8)
        prev_row = jnp.where(r == 0, 0.0, x_ref[0, pl.ds(p0, 8), :][7:8, :])
        next_row = jnp.where(r == n_tiles - 1, 0.0, x_ref[0, pl.ds(n0, 8), :][0:1, :])
        xm1, xp1 = _shift_rows(x, prev_row, next_row)
        y = xm1 * w[0:1, :] + x * w[1:2, :] + xp1 * w[2:3, :]
        y = y * _sigmoid(y)
        if l2_scale is not None:
            y = y * lax.rsqrt(jnp.sum(y * y, axis=-1, keepdims=True) + NORM_EPS) * l2_scale
        dst_ref[pl.ds(t0, rows), :] = y
        return carry

    lax.fori_loop(0, n_tiles, body, 0)


def _unit_tri_inverses(mats):
    c = mats[0].shape[0]
    ri = lax.broadcasted_iota(jnp.int32, (c, 2 * c), 0)
    ci = lax.broadcasted_iota(jnp.int32, (c, 2 * c), 1)
    eye = jnp.where(jnp.logical_or(ri == ci, ri + c == ci), 1.0, 0.0)

    def blockdiag(x):
        return jnp.concatenate([jnp.where(ci < c, x, 0.0), jnp.where(ci >= c, x, 0.0)], axis=0).astype(_BF16)

    xs = [eye - a for a in mats]
    ps = [_dot(a.astype(_BF16), blockdiag(a)) for a in mats]
    yield
    power = 2
    while 2 * power < c:
        xps = [_dot(jnp.concatenate([x, p], axis=0).astype(_BF16), blockdiag(p)) for x, p in zip(xs, ps)]
        xs = [x + xp[:c] for x, xp in zip(xs, xps)]
        ps = [xp[c:] for xp in xps]
        power *= 2
        yield
    return [x + _dot(x.astype(_BF16), blockdiag(p)) for x, p in zip(xs, ps)]


def _interleave(*stages):
    results = [None] * len(stages)
    live = list(range(len(stages)))
    while live:
        for i in list(live):
            try:
                next(stages[i])
            except StopIteration as stop:
                results[i] = stop.value
                live.remove(i)
    return results


GDN_GROUP = 8
GDN_SET_FIELDS = 9


def _gdn_set_shapes():
    g, c, d = GDN_GROUP, GDN_CHUNK, HEAD_DIM
    per_dir = [pltpu.VMEM((g * 4 * c, d), _BF16), pltpu.VMEM((g * d, d), _BF16), pltpu.VMEM((g * c, d), _F32),
               pltpu.VMEM((g, d), _F32)]
    return per_dir * 2 + [pltpu.VMEM((g * c, 2 * c), _BF16)]


def _gdn_prepare_group(cidx_lists, *, head, qn, kn, vn, slab_ref, grow_ref, ring_set):
    c, d = GDN_CHUNK, HEAD_DIM
    ri = lax.broadcasted_iota(jnp.int32, (c, 2 * c), 0)
    ci = lax.broadcasted_iota(jnp.int32, (c, 2 * c), 1)
    left = ci < c
    cil = jnp.where(left, ci, ci - c)
    fwd_i = left.astype(jnp.int32)
    incl = (ri - cil) * (2 * fwd_i - 1) >= 0
    strict = (ri - cil) * (2 * fwd_i - 1) > 0
    lane = lax.broadcasted_iota(jnp.int32, (c, LANES), 1)
    zero_k = jnp.zeros((c, d), _BF16)
    zero_rhs = jnp.zeros((c, 2 * d), _BF16)
    pairs = []
    for slot, (cf, cb) in enumerate(zip(*cidx_lists)):
        halves = []
        for forward, cidx in ((True, cf), (False, cb)):
            r0 = pl.multiple_of(cidx * c, c)
            slab = slab_ref[0, pl.ds(r0, c), :]
            g_lane = head if forward else GDN_HEADS + head
            b_lane = 2 * GDN_HEADS + g_lane
            gcol = jnp.sum(jnp.where(lane == g_lane, slab, 0.0), axis=-1, keepdims=True)
            bcol = jnp.sum(jnp.where(lane == b_lane, slab, 0.0), axis=-1, keepdims=True)
            halves.append(dict(qc=qn[pl.ds(r0, c), :], kc=kn[pl.ds(r0, c), :], vc=vn[pl.ds(r0, c), :],
                               gcol=gcol, bcol=bcol, glast=gcol[c - 1:c, :] if forward else gcol[0:1, :]))
        f, b = halves
        k16f, k16b = f["kc"].astype(_BF16), b["kc"].astype(_BF16)
        lhs = jnp.concatenate([jnp.concatenate([f["qc"], b["qc"]], axis=1),
                               jnp.concatenate([f["kc"], b["kc"]], axis=1)], axis=0).astype(_BF16)
        rhs_t = jnp.concatenate([jnp.concatenate([k16f, zero_k], axis=1),
                                 jnp.concatenate([zero_k, k16b], axis=1)], axis=0)
        prod = _dot_nt(lhs, rhs_t)
        gcol2 = jnp.where(left, f["gcol"], b["gcol"])
        bcol2 = jnp.where(left, f["bcol"], b["bcol"])
        grow2 = grow_ref[0, 0, pl.ds(cf, 1), :]
        decay = jnp.where(incl, jnp.exp(jnp.where(incl, gcol2 - grow2, 0.0)), 0.0)
        pairs.append(dict(slot=slot, f=f, b=b, qk=prod[:c], decay=decay,
                          a=jnp.where(strict, prod[c:] * decay * bcol2, 0.0)))
    yield
    tinvs = yield from _unit_tri_inverses([pr["a"] for pr in pairs])
    yield
    uws = []
    for pr, tinv in zip(pairs, tinvs):
        rhs = []
        for hv in (pr["f"], pr["b"]):
            hv["eg"] = jnp.exp(hv["gcol"])
            rhs.append(jnp.concatenate([hv["vc"] * hv["bcol"], hv["kc"] * hv["bcol"] * hv["eg"]],
                                       axis=1).astype(_BF16))
        rhs2 = jnp.concatenate([jnp.concatenate([rhs[0], zero_rhs], axis=1),
                                jnp.concatenate([zero_rhs, rhs[1]], axis=1)], axis=0)
        uws.append(_dot(tinv.astype(_BF16), rhs2))
    yield
    mcs = []
    for pr, uw in zip(pairs, uws):
        for hv, off in ((pr["f"], 0), (pr["b"], 2 * d)):
            kdt = (hv["kc"] * jnp.exp(hv["glast"] - hv["gcol"])).T.astype(_BF16)
            mcs.append(_dot(kdt, uw[:, off:off + 2 * d].astype(_BF16)))
    yield
    intra_ref = ring_set[-1]
    for i, (pr, uw) in enumerate(zip(pairs, uws)):
        slot = pr["slot"]
        for hv, off, mc, ring in ((pr["f"], 0, mcs[2 * i], ring_set[0:4]), (pr["b"], 2 * d, mcs[2 * i + 1], ring_set[4:8])):
            lhs_ref, c_ref, u_ref, dec_ref = ring
            lhs_ref[pl.ds(slot * 4 * c, 2 * c), :] = (-mc[:, d:]).astype(_BF16)
            lhs_ref[pl.ds(slot * 4 * c + 2 * c, c), :] = uw[:, off + d:off + 2 * d].astype(_BF16)
            lhs_ref[pl.ds(slot * 4 * c + 3 * c, c), :] = (hv["qc"] * hv["eg"]).astype(_BF16)
            c_ref[pl.ds(slot * 2 * c, 2 * c), :] = mc[:, :d].astype(_BF16)
            u_ref[pl.ds(slot * c, c), :] = uw[:, off:off + d]
            dec_ref[pl.ds(slot, 1), :] = jnp.broadcast_to(jnp.exp(hv["glast"]), (1, d))
        intra_ref[pl.ds(slot * c, c), :] = jnp.where(incl, pr["qk"] * pr["decay"], 0.0).astype(_BF16)


def _gdn_scan_group(states, cidx_lists, ring_set, out_refs):
    c = GDN_CHUNK
    rings = (ring_set[0:4], ring_set[4:8])
    intra_ref = ring_set[-1]
    zero_v = jnp.zeros((c, HEAD_DIM), _BF16)
    for slot in range(GDN_GROUP):
        prods = [_dot(ring[0][pl.ds(slot * 4 * c, 4 * c), :], s.astype(_BF16)) for ring, s in zip(rings, states)]
        states = [s * ring[3][pl.ds(slot, 1), :] + p[:2 * c] + ring[1][pl.ds(slot * 2 * c, 2 * c), :].astype(_F32)
                  for ring, s, p in zip(rings, states, prods)]
        intra = intra_ref[pl.ds(slot * c, c), :]
        for forward, ring, cidxs, out_ref, p in zip((True, False), rings, cidx_lists, out_refs, prods):
            v16 = (ring[2][pl.ds(slot * c, c), :] - p[2 * c:3 * c]).astype(_BF16)
            v_pad = jnp.concatenate([v16, zero_v] if forward else [zero_v, v16], axis=0)
            r0 = pl.multiple_of(cidxs[slot] * c, c)
            out_ref[pl.ds(r0, c), :] = p[3 * c:] + _dot(intra, v_pad)
        yield
    return tuple(states)


def _gdn_kernel(q_ref, k_ref, v_ref, cwq_ref, cwk_ref, cwv_ref, slab_ref, grow_ref, z_ref, nw_ref,
                o_ref, qn, kn, vn, of, ob, *ring_refs):
    head = pl.program_id(1)
    seq = q_ref.shape[1]
    n_chunks = seq // GDN_CHUNK
    n_groups = n_chunks // GDN_GROUP
    _conv_silu(q_ref, cwq_ref, qn, l2_scale=HEAD_DIM ** -0.5)
    _conv_silu(k_ref, cwk_ref, kn, l2_scale=1.0)
    _conv_silu(v_ref, cwv_ref, vn, l2_scale=None)

    nf = GDN_SET_FIELDS
    sets = [tuple(ring_refs[s * nf:(s + 1) * nf]) for s in range(2)]

    def groups(n):
        return ([n * GDN_GROUP + j for j in range(GDN_GROUP)],
                [n_chunks - 1 - n * GDN_GROUP - j for j in range(GDN_GROUP)])

    def prepare(n, ring_set):
        return _gdn_prepare_group(groups(n), head=head, qn=qn, kn=kn, vn=vn, slab_ref=slab_ref,
                                  grow_ref=grow_ref, ring_set=ring_set)

    def scan(n, states, ring_set):
        return _gdn_scan_group(states, groups(n), ring_set, (of, ob))

    def stage(n, states, scan_set, prep_set):
        return _interleave(scan(n, states, scan_set), prepare(n + 1, prep_set))[0]

    def stage_pair(m, states):
        n = 2 * m
        states = stage(n, states, sets[0], sets[1])
        return stage(n + 1, states, sets[1], sets[0])

    zero = jnp.zeros((HEAD_DIM, HEAD_DIM), _F32)
    _interleave(prepare(0, sets[0]))
    states = lax.fori_loop(0, n_groups // 2 - 1, stage_pair, (zero, zero))
    states = stage(n_groups - 2, states, sets[0], sets[1])
    _interleave(scan(n_groups - 1, states, sets[1]))

    rows = 256

    def finish(r, carry):
        sl = pl.ds(pl.multiple_of(r * rows, rows), rows)
        o = of[sl, :] + ob[sl, :]
        o = o * lax.rsqrt(jnp.mean(o * o, axis=-1, keepdims=True) + NORM_EPS) * nw_ref[...]
        z = z_ref[0, sl, :].astype(_F32)
        o_ref[0, sl, :] = (o * (z * _sigmoid(z))).astype(o_ref.dtype)
        return carry

    lax.fori_loop(0, seq // rows, finish, 0)


def _gdn_mixer(qkv, z, slab, conv_w, norm_w):
    bsz, seq, _ = qkv.shape
    c = GDN_CHUNK
    nc = seq // c
    h = GDN_HEADS
    slab_t = slab.transpose(0, 2, 1).reshape(bsz, LANES, nc, c)
    grow = jnp.concatenate([slab_t[:, :h], slab_t[:, h:2 * h, ::-1]], axis=-1)
    col = lambda off: pl.BlockSpec((1, seq, HEAD_DIM), functools.partial(lambda b, hh, o: (b, 0, o + hh), o=off))
    cw = lambda off: pl.BlockSpec((3, HEAD_DIM), functools.partial(lambda b, hh, o: (0, o + hh), o=off))
    return pl.pallas_call(
        _gdn_kernel,
        grid=(bsz, h),
        in_specs=[col(0), col(h), col(2 * h), cw(0), cw(h), cw(2 * h),
                  pl.BlockSpec((1, seq, LANES), lambda b, hh: (b, 0, 0)),
                  pl.BlockSpec((1, 1, nc, 2 * c), lambda b, hh: (b, hh, 0, 0)),
                  col(0),
                  pl.BlockSpec((1, HEAD_DIM), lambda b, hh: (0, 0))],
        out_specs=col(0),
        out_shape=jax.ShapeDtypeStruct((bsz, seq, GDN_WIDTH), _BF16),
        scratch_shapes=([pltpu.VMEM((seq, HEAD_DIM), _F32)] * 5
                        + _gdn_set_shapes() * 2),
        compiler_params=_params("arbitrary", "arbitrary"),
        name="gdn_mixer",
    )(qkv, qkv, qkv, conv_w, conv_w, conv_w, slab, grow, z, norm_w.reshape(1, HEAD_DIM).astype(_F32))


ATT_RADIUS = 64
ATT_BQ = 128
ATT_WIDTH_KEYS = ATT_BQ + 2 * ATT_RADIUS
ATT_GROUP = 8
ATT_SPLIT = 4
assert all(w // (2 * d) == ATT_RADIUS for w, d in DILATED_PATTERNS)
assert tuple(d for _, d in DILATED_PATTERNS) == (1, ATT_SPLIT, ATT_SPLIT * ATT_SPLIT)


def _attn_blocks(qs, kws, vws, biases):
    scale = HEAD_DIM ** -0.5
    ss = [_dot_nt(q, kw) * scale + b for q, kw, b in zip(qs, kws, biases)]
    ms = [jnp.max(s, axis=-1, keepdims=True) for s in ss]
    es = [jnp.exp(s - m) for s, m in zip(ss, ms)]
    ls = [jnp.sum(e, axis=-1, keepdims=True) for e in es]
    accs = [_dot(e.astype(_BF16), vw) for e, vw in zip(es, vws)]
    return accs, ms, ls


def _attn_merge(acc_a, m_a, l_a, acc_b, m_b, l_b):
    m = jnp.maximum(m_a, m_b)
    wa = jnp.exp(m_a - m)
    wb = jnp.exp(m_b - m)
    return wa * acc_a + wb * acc_b, m, wa * l_a + wb * l_b


def _attn_kernel(slope_ref, q_ref, k_ref, v_ref, o_ref, nat, xq, xk, xv, acc4, m4, l4, acc1, m1, l1, bias_scr):
    seq = q_ref.shape[1]
    bq, width, radius, split, group = ATT_BQ, ATT_WIDTH_KEYS, ATT_RADIUS, ATT_SPLIT, ATT_GROUP
    n4 = seq // split
    n16 = n4 // split
    rep = lambda x: jnp.broadcast_to(x, (bq, HEAD_DIM))

    slope = slope_ref[0][:, 0:1]
    rel = (lax.broadcasted_iota(jnp.int32, (bq, width), 1) - lax.broadcasted_iota(jnp.int32, (bq, width), 0))
    for p, (_, dilation) in enumerate(DILATED_PATTERNS):
        for pos in range(3):
            dist = jnp.abs(rel - pos * radius)
            bias_scr[p, pos] = jnp.where(dist <= radius, -(slope * float(dilation)) * dist.astype(_F32), NEG_INF)

    def window(i0, n):
        ws = pl.multiple_of(jnp.clip(i0 - radius, 0, n - width), radius)
        return ws, (i0 - ws) // radius

    rows = 512
    for src, dst in ((q_ref, xq), (k_ref, xk), (v_ref, xv)):
        def widen(r, carry, src=src):
            sl = pl.ds(pl.multiple_of(r * rows, rows), rows)
            nat[sl, :] = src[0, sl, :].astype(_F32)
            return carry
        lax.fori_loop(0, seq // rows, widen, 0)
        for cls in range(split):
            def gather(r, carry, dst=dst, cls=cls):
                r0 = pl.multiple_of(r * rows, rows)
                dst[cls, pl.ds(r0, rows), :] = nat[pl.ds(cls + split * r0, rows, stride=split), :]
                return carry
            lax.fori_loop(0, n4 // rows, gather, 0)

    per_step = group // split

    def dil4(i, carry):
        qs, kws, vws, biases, q_rows = [], [], [], [], []
        for j in range(per_step):
            i0 = pl.multiple_of((i * per_step + j) * bq, bq)
            ws, pos = window(i0, n4)
            for c in range(split):
                q_rows.append((c, pl.ds(i0, bq)))
                qs.append(xq[c, pl.ds(i0, bq), :].astype(_BF16))
                kws.append(xk[c, pl.ds(ws, width), :].astype(_BF16))
                vws.append(xv[c, pl.ds(ws, width), :].astype(_BF16))
                biases.append(bias_scr[1, pos])
        accs, ms, ls = _attn_blocks(qs, kws, vws, biases)
        for (c, rows_), acc, m, l in zip(q_rows, accs, ms, ls):
            acc4[c, rows_, :] = acc
            m4[c, rows_, :] = rep(m)
            l4[c, rows_, :] = rep(l)
        return carry

    lax.fori_loop(0, n4 // (bq * per_step), dil4, 0)

    def dil16(i, carry):
        qs, kws, vws, biases, q_rows = [], [], [], [], []
        for j in range(per_step):
            blk = i * per_step + j
            sub = blk // (n16 // bq)
            i0 = pl.multiple_of((blk % (n16 // bq)) * bq, bq)
            ws, pos = window(i0, n16)
            k_rows = pl.ds(sub + split * ws, width, stride=split)
            for c in range(split):
                q_rows.append((c, pl.ds(sub + split * i0, bq, stride=split)))
                qs.append(xq[c, q_rows[-1][1], :].astype(_BF16))
                kws.append(xk[c, k_rows, :].astype(_BF16))
                vws.append(xv[c, k_rows, :].astype(_BF16))
                biases.append(bias_scr[2, pos])
        accs, ms, ls = _attn_blocks(qs, kws, vws, biases)
        for (c, rows_), acc_b, m_b, l_b in zip(q_rows, accs, ms, ls):
            acc, m, l = _attn_merge(acc4[c, rows_, :], m4[c, rows_, :], l4[c, rows_, :], acc_b, m_b, l_b)
            acc4[c, rows_, :] = acc
            m4[c, rows_, :] = m
            l4[c, rows_, :] = l
        return carry

    lax.fori_loop(0, split * (n16 // bq) // per_step, dil16, 0)

    def dil1(i, carry):
        qs, kws, vws, biases, i0s = [], [], [], [], []
        for j in range(group):
            i0 = pl.multiple_of((i * group + j) * bq, bq)
            ws, pos = window(i0, seq)
            i0s.append(i0)
            qs.append(q_ref[0, pl.ds(i0, bq), :])
            kws.append(k_ref[0, pl.ds(ws, width), :])
            vws.append(v_ref[0, pl.ds(ws, width), :])
            biases.append(bias_scr[0, pos])
        accs, ms, ls = _attn_blocks(qs, kws, vws, biases)
        for j in range(group):
            acc1[pl.ds(i0s[j], bq), :] = accs[j]
            m1[pl.ds(i0s[j], bq), :] = rep(ms[j])
            l1[pl.ds(i0s[j], bq), :] = rep(ls[j])
        return carry

    lax.fori_loop(0, seq // (bq * group), dil1, 0)

    for cls in range(split):
        def finish(r, carry, cls=cls):
            r0 = pl.multiple_of(r * bq, bq)
            nat_rows = pl.ds(cls + split * r0, bq, stride=split)
            acc, _, l = _attn_merge(acc1[nat_rows, :], m1[nat_rows, :], l1[nat_rows, :],
                                    acc4[cls, pl.ds(r0, bq), :], m4[cls, pl.ds(r0, bq), :], l4[cls, pl.ds(r0, bq), :])
            nat[nat_rows, :] = acc / l
            return carry
        lax.fori_loop(0, n4 // bq, finish, 0)

    def narrow(r, carry):
        sl = pl.ds(pl.multiple_of(r * rows, rows), rows)
        o_ref[0, sl, :] = nat[sl, :].astype(o_ref.dtype)
        return carry

    lax.fori_loop(0, seq // rows, narrow, 0)


def _dilated_attention(qkv, slopes):
    bsz, seq, _ = qkv.shape
    h = ATT_HEADS
    n4 = seq // ATT_SPLIT
    n16 = n4 // ATT_SPLIT
    assert n16 % ATT_BQ == 0 and n16 >= ATT_WIDTH_KEYS and seq % (ATT_BQ * ATT_GROUP) == 0
    assert ATT_GROUP % ATT_SPLIT == 0 and (ATT_SPLIT * n16 // ATT_BQ) % (ATT_GROUP // ATT_SPLIT) == 0
    col = lambda off: pl.BlockSpec((1, seq, HEAD_DIM), functools.partial(lambda b, hh, o: (b, 0, o + hh), o=off))
    slab = pltpu.VMEM((ATT_SPLIT, n4, HEAD_DIM), _F32)
    nat = pltpu.VMEM((seq, HEAD_DIM), _F32)
    return pl.pallas_call(
        _attn_kernel,
        grid=(bsz, h),
        in_specs=[pl.BlockSpec((1, 1, LANES), lambda b, hh: (hh, 0, 0)), col(0), col(h), col(2 * h)],
        out_specs=col(0),
        out_shape=jax.ShapeDtypeStruct((bsz, seq, ATT_WIDTH), _BF16),
        scratch_shapes=[nat, slab, slab, slab, slab, slab, slab, nat, nat, nat,
                        pltpu.VMEM((len(DILATED_PATTERNS), 3, ATT_BQ, ATT_WIDTH_KEYS), _F32)],
        compiler_params=_params("arbitrary", "arbitrary"),
        name="dilated_attention",
    )(slopes, qkv, qkv, qkv)


FFN_EPILOGUE_ROWS = 32

def _ffn_up_kernel(a_ref, wg_ref, wu_ref, cg_ref, cu_ref, o_ref, wcat, y_even, y_odd, *, n_tiles, tiles_per_seq, halo):
    s = pl.program_id(0)
    n_steps = pl.num_programs(0) - 1
    tn = wg_ref.shape[1]
    tm = o_ref.shape[0]

    @pl.when(jnp.logical_and(s % n_tiles == 0, s < n_steps))
    def _():
        _cast_rows(wg_ref, wcat.at[:, pl.ds(0, tn)])
        _cast_rows(wu_ref, wcat.at[:, pl.ds(tn, tn)])

    @pl.when(s == 0)
    def _():
        y_odd[...] = jnp.zeros(y_odd.shape, y_odd.dtype)

    tile = jnp.maximum(s - 1, 0) % n_tiles
    first = (tile % tiles_per_seq) == 0
    last = (tile % tiles_per_seq) == tiles_per_seq - 1

    def step(y_new, y_old):
        cw = jnp.concatenate([cg_ref[...], cu_ref[...]], axis=1)
        rows = FFN_EPILOGUE_ROWS
        proj = _dot(a_ref[...], wcat[...])
        for c in range(tm // rows):
            r0 = halo + c * rows
            y = y_old[pl.ds(r0, rows), :]
            prev_row = y_old[pl.ds(r0 - 8, 8), :][7:8, :]
            next_row = y_old[pl.ds(r0 + rows, 8), :][0:1, :]
            if c == 0:
                prev_row = jnp.where(first, 0.0, prev_row)
            if c == tm // rows - 1:
                next_row = jnp.where(last, 0.0, next_row)
            ym1, yp1 = _shift_rows(y, prev_row, next_row)
            conv = ym1 * cw[0:1, :] + y * cw[1:2, :] + yp1 * cw[2:3, :]
            gate = conv[:, :tn]
            up = conv[:, tn:]
            o_ref[pl.ds(c * rows, rows), :] = (gate * _sigmoid(gate) * up).astype(o_ref.dtype)
        y_new[...] = proj

    @pl.when(s % 2 == 0)
    def _():
        step(y_even, y_odd)

    @pl.when(s % 2 == 1)
    def _():
        step(y_odd, y_even)


def _ffn_up(u_pad, w_up, conv_w, seq, pad_rows, tm=1024, tn=256):
    m, k = u_pad.shape[0] - 2 * pad_rows, u_pad.shape[1]
    ffn = w_up.shape[1] // 2
    halo = BF16_ROWS
    assert ffn % tn == 0 and seq % tm == 0 and pad_rows >= halo
    nj = ffn // tn
    n_tiles = m // tm
    n_steps = nj * n_tiles
    assert tm % halo == 0 and pad_rows % halo == 0
    col = lambda s: jnp.minimum(s // n_tiles, nj - 1)
    col_prev = lambda s: jnp.maximum(s - 1, 0) // n_tiles
    a_spec = pl.BlockSpec((pl.Element(tm + 2 * halo), pl.Element(k)),
                          lambda s: (((s % n_tiles) * (tm // halo) + (pad_rows // halo - 1)) * halo, 0))
    return pl.pallas_call(
        functools.partial(_ffn_up_kernel, n_tiles=n_tiles, tiles_per_seq=seq // tm, halo=halo),
        grid=(n_steps + 1,),
        in_specs=[a_spec,
                  pl.BlockSpec((k, tn), lambda s: (0, col(s))),
                  pl.BlockSpec((k, tn), lambda s: (0, col(s) + nj)),
                  pl.BlockSpec((3, tn), lambda s: (0, col_prev(s))),
                  pl.BlockSpec((3, tn), lambda s: (0, col_prev(s) + nj))],
        out_specs=pl.BlockSpec((tm, tn), lambda s: (jnp.maximum(s - 1, 0) % n_tiles, col_prev(s))),
        out_shape=jax.ShapeDtypeStruct((m, ffn), _BF16),
        scratch_shapes=[pltpu.VMEM((k, 2 * tn), _BF16)] + [pltpu.VMEM((tm + 2 * halo, 2 * tn), _F32)] * 2,
        compiler_params=_params("arbitrary"),
        name="ffn_up_conv_glu",
    )(u_pad, w_up, w_up, conv_w, conv_w)


def _ple_kernel(a_ref, wg_ref, p_ref, wp_ref, h_ref, o_ref, wgb, wpb):
    @pl.when(pl.program_id(1) == 0)
    def _():
        _cast_rows(wg_ref, wgb)
        _cast_rows(wp_ref, wpb)

    gate = _sigmoid(_dot(a_ref[...], wgb[...]))
    emb = _dot(p_ref[...], wpb[...])
    o_ref[...] = h_ref[...] + gate * emb


def _ple(u, w_gate, p, w_proj, h, *, tm, tn):
    m, k = u.shape
    n = w_gate.shape[1]
    kp = p.shape[1]
    return pl.pallas_call(
        _ple_kernel,
        grid=(n // tn, m // tm),
        in_specs=[pl.BlockSpec((tm, k), lambda j, i: (i, 0)),
                  pl.BlockSpec((k, tn), lambda j, i: (0, j)),
                  pl.BlockSpec((tm, kp), lambda j, i: (i, 0)),
                  pl.BlockSpec((kp, tn), lambda j, i: (0, j)),
                  pl.BlockSpec((tm, tn), lambda j, i: (i, j))],
        out_specs=pl.BlockSpec((tm, tn), lambda j, i: (i, j)),
        out_shape=jax.ShapeDtypeStruct((m, n), _F32),
        scratch_shapes=[pltpu.VMEM((k, tn), _BF16), pltpu.VMEM((kp, tn), _BF16)],
        compiler_params=_params("arbitrary", "arbitrary"),
        name="ple_gate",
    )(u, w_gate, p, w_proj, h)


def kernel(x, p, attn_norm, w_in, gdn_conv, gdn_a_log, gdn_dt_bias, gdn_out_norm, w_out, ffn_norm, w_up,
           ffn_conv, w_down, ple_norm, w_ple_gate, w_ple_proj, final_norm):
    bsz, seq, d_model = x.shape
    depth = w_in.shape[0]
    m = bsz * seq
    qkv_cols = 3 * GDN_WIDTH
    ab_off = qkv_cols + GDN_WIDTH
    att_off = ab_off + 4 * GDN_HEADS
    slopes = jnp.exp2(-ALIBI_MAX_BIAS * (jnp.arange(ATT_HEADS, dtype=_F32) + 1.0) / ATT_HEADS)
    slopes = jnp.broadcast_to(slopes[:, None, None], (ATT_HEADS, 1, LANES))

    h = x.reshape(m, d_model)
    for i in range(depth):
        u = _rmsnorm(h, attn_norm[i], _BF16)
        wt = jnp.swapaxes(w_in[i], 0, 1)
        tm, tn = MM_TM, MM_TN
        gdn_qkv = _matmul_nt(u, wt, n_cols=qkv_cols, row_off=0, tm=tm, tn=tn, out_dtype=_F32, name="proj_gdn_qkv")
        gdn_z = _matmul_nt(u, wt, n_cols=GDN_WIDTH, row_off=qkv_cols, tm=tm, tn=tn, out_dtype=_BF16,
                           name="proj_gdn_z")
        ab = _matmul_nt(u, wt, n_cols=LANES, row_off=ab_off, tm=tm, tn=LANES, out_dtype=_F32, name="proj_gdn_ab")
        att_qkv = _matmul_nt(u, wt, n_cols=3 * ATT_WIDTH, row_off=att_off, tm=tm, tn=tn, out_dtype=_BF16,
                             name="proj_att_qkv")

        slab = _gdn_gates(ab, gdn_a_log[i], gdn_dt_bias[i]).reshape(bsz, seq, LANES)
        o_gdn = _gdn_mixer(gdn_qkv.reshape(bsz, seq, qkv_cols), gdn_z.reshape(bsz, seq, GDN_WIDTH), slab,
                           gdn_conv[i], gdn_out_norm[i]).reshape(m, GDN_WIDTH)

        att3 = att_qkv.reshape(bsz, seq, 3 * ATT_WIDTH)
        o_att = _dilated_attention(att3, slopes).reshape(m, ATT_WIDTH)

        h = _matmul([o_gdn, o_att], w_out[i], n_cols=d_model, tm=tm, tn=tn, out_dtype=_F32, residual=h,
                    name="out_proj")

        norm_tm = 256
        u_pad = _rmsnorm(h, ffn_norm[i], _BF16, tm=norm_tm, pad_tiles=1)
        act = _ffn_up(u_pad, w_up[i], ffn_conv[i], seq, pad_rows=norm_tm)
        h = _matmul([act], w_down[i].astype(_BF16), n_cols=d_model, tm=MM_TM_LONG_K, tn=tn, out_dtype=_F32,
                    residual=h, name="ffn_down")

        u = _rmsnorm(h, ple_norm[i], _BF16)
        h = _ple(u, w_ple_gate[i], p[i].reshape(m, -1).astype(_BF16), w_ple_proj[i], h, tm=tm, tn=tn)
    return _rmsnorm(h, final_norm, x.dtype).reshape(bsz, seq, d_model)
```

```python
import functools

import jax
import jax.numpy as jnp
from jax import lax
from jax.experimental import pallas as pl
from jax.experimental.pallas import tpu as pltpu

HEAD_DIM = 128
GDN_HEADS = 16
ATT_HEADS = 16
GDN_WIDTH = GDN_HEADS * HEAD_DIM
ATT_WIDTH = ATT_HEADS * HEAD_DIM
GDN_CHUNK = 64
DILATED_PATTERNS = ((128, 1), (512, 4), (2048, 16))
ALIBI_MAX_BIAS = 8.0
NORM_EPS = 1e-6
NEG_INF = -1e30

LANES = 128
SUBLANES = 8
BF16_ROWS = 16
VMEM_LIMIT_BYTES = 56 * 1024 * 1024
MM_TM = 1024
MM_TN = 512
MM_TM_LONG_K = 512

_F32 = jnp.float32
_BF16 = jnp.bfloat16


def _params(*semantics):
    return pltpu.CompilerParams(dimension_semantics=semantics, vmem_limit_bytes=VMEM_LIMIT_BYTES)


def _sigmoid(x):
    return 1.0 / (1.0 + jnp.exp(-x))


def _dot(a, b):
    return jnp.dot(a, b, preferred_element_type=_F32)


def _dot_nt(a, b):
    return lax.dot_general(a, b, (((1,), (1,)), ((), ())), preferred_element_type=_F32)


def _rmsnorm_kernel(x_ref, w_ref, o_ref, *, pad_tiles):
    x = x_ref[...].astype(_F32)
    ms = jnp.mean(x * x, axis=-1, keepdims=True)
    y = x * lax.rsqrt(ms + NORM_EPS) * w_ref[...]
    if pad_tiles:
        i = pl.program_id(0)
        inside = jnp.logical_and(i >= pad_tiles, i < pl.num_programs(0) - pad_tiles)
        y = jnp.where(inside, y, 0.0)
    o_ref[...] = y.astype(o_ref.dtype)


def _rmsnorm(x, w, out_dtype, tm=256, pad_tiles=0):
    m, d = x.shape
    nt = m // tm
    return pl.pallas_call(
        functools.partial(_rmsnorm_kernel, pad_tiles=pad_tiles),
        grid=(nt + 2 * pad_tiles,),
        in_specs=[pl.BlockSpec((tm, d), lambda i: (jnp.clip(i - pad_tiles, 0, nt - 1), 0)),
                  pl.BlockSpec((1, d), lambda i: (0, 0))],
        out_specs=pl.BlockSpec((tm, d), lambda i: (i, 0)),
        out_shape=jax.ShapeDtypeStruct((m + 2 * pad_tiles * tm, d), out_dtype),
        compiler_params=_params("arbitrary"),
        name="rmsnorm",
    )(x, w.reshape(1, d).astype(_F32))


def _cast_rows(src_ref, dst_ref, rows=256):
    def body(r, carry):
        sl = pl.ds(pl.multiple_of(r * rows, rows), rows)
        dst_ref[sl, :] = src_ref[sl, :].astype(dst_ref.dtype)
        return carry
    lax.fori_loop(0, src_ref.shape[0] // rows, body, 0)


def _mm_kernel(*refs, n_a, has_res, cast_w):
    a_refs = refs[:n_a]
    w_refs = refs[n_a:2 * n_a]
    pos = 2 * n_a
    res_ref = refs[pos] if has_res else None
    pos += int(has_res)
    o_ref = refs[pos]
    wb_refs = refs[pos + 1:pos + 1 + n_a] if cast_w else w_refs
    if cast_w:
        @pl.when(pl.program_id(1) == 0)
        def _():
            for w_ref, wb_ref in zip(w_refs, wb_refs):
                _cast_rows(w_ref, wb_ref)
    acc = None
    for a_ref, wb_ref in zip(a_refs, wb_refs):
        part = _dot(a_ref[...], wb_ref[...])
        acc = part if acc is None else acc + part
    if has_res:
        acc = acc + res_ref[...]
    o_ref[...] = acc.astype(o_ref.dtype)


def _mm_nt_kernel(a_ref, wt_ref, o_ref, wb_ref):
    @pl.when(pl.program_id(1) == 0)
    def _():
        _cast_rows(wt_ref, wb_ref, rows=SUBLANES * 8)

    o_ref[...] = _dot_nt(a_ref[...], wb_ref[...]).astype(o_ref.dtype)


def _matmul_nt(a, wt, *, n_cols, row_off, tm, tn, out_dtype, name):
    m, k = a.shape
    assert n_cols % tn == 0 and m % tm == 0 and row_off % SUBLANES == 0 and tn % SUBLANES == 0
    return pl.pallas_call(
        _mm_nt_kernel,
        grid=(n_cols // tn, m // tm),
        in_specs=[pl.BlockSpec((tm, k), lambda j, i: (i, 0)),
                  pl.BlockSpec((pl.Element(tn), pl.Element(k)),
                               lambda j, i: ((j * (tn // SUBLANES) + row_off // SUBLANES) * SUBLANES, 0))],
        out_specs=pl.BlockSpec((tm, tn), lambda j, i: (i, j)),
        out_shape=jax.ShapeDtypeStruct((m, n_cols), out_dtype),
        scratch_shapes=[pltpu.VMEM((tn, k), _BF16)],
        compiler_params=_params("arbitrary", "arbitrary"),
        name=name,
    )(a, wt)


def _matmul(a_list, w, *, n_cols, col_off=0, tm, tn, out_dtype, residual=None, name):
    m = a_list[0].shape[0]
    assert col_off % tn == 0 and n_cols % tn == 0 and m % tm == 0
    off = col_off // tn
    cast_w = w.dtype != _BF16
    in_specs, scratch = [], []
    for a in a_list:
        in_specs.append(pl.BlockSpec((tm, a.shape[1]), lambda j, i: (i, 0)))
    row_blk = 0
    for a in a_list:
        kk = a.shape[1]
        assert all(b.shape[1] == kk for b in a_list)
        in_specs.append(pl.BlockSpec((kk, tn), functools.partial(lambda j, i, rb: (rb, j + off), rb=row_blk)))
        if cast_w:
            scratch.append(pltpu.VMEM((kk, tn), _BF16))
        row_blk += 1
    args = list(a_list) + [w] * len(a_list)
    if residual is not None:
        in_specs.append(pl.BlockSpec((tm, tn), lambda j, i: (i, j)))
        args.append(residual)
    return pl.pallas_call(
        functools.partial(_mm_kernel, n_a=len(a_list), has_res=residual is not None, cast_w=cast_w),
        grid=(n_cols // tn, m // tm),
        in_specs=in_specs,
        out_specs=pl.BlockSpec((tm, tn), lambda j, i: (i, j)),
        out_shape=jax.ShapeDtypeStruct((m, n_cols), out_dtype),
        scratch_shapes=scratch,
        compiler_params=_params("arbitrary", "arbitrary"),
        name=name,
    )(*args)


def _gate_kernel(ab_ref, alog_ref, dtb_ref, o_ref):
    x = ab_ref[...]
    rows = x.shape[0]
    xa = x + dtb_ref[...]
    softplus = jnp.maximum(xa, 0.0) + jnp.log(1.0 + jnp.exp(-jnp.abs(xa)))
    g = -jnp.exp(alog_ref[...]) * softplus
    beta = _sigmoid(x)
    c = GDN_CHUNK
    ri = lax.broadcasted_iota(jnp.int32, (c, c), 0)
    ci = lax.broadcasted_iota(jnp.int32, (c, c), 1)
    lower = (ri >= ci).astype(_F32)
    upper = (ri <= ci).astype(_F32)
    lane = lax.broadcasted_iota(jnp.int32, (c, LANES), 1)
    for n in range(rows // c):
        gc = g[n * c:(n + 1) * c]
        prefix = jnp.dot(lower, gc, precision=lax.Precision.HIGHEST, preferred_element_type=_F32)
        suffix = jnp.dot(upper, gc, precision=lax.Precision.HIGHEST, preferred_element_type=_F32)
        out = jnp.where(lane < GDN_HEADS, prefix,
                        jnp.where(lane < 2 * GDN_HEADS, suffix,
                                  jnp.where(lane < 4 * GDN_HEADS, beta[n * c:(n + 1) * c], 0.0)))
        o_ref[n * c:(n + 1) * c, :] = out


def _gdn_gates(ab, a_log, dt_bias, tm=512):
    m = ab.shape[0]
    pad = LANES - 2 * GDN_HEADS
    alog_vec = jnp.pad(a_log.reshape(-1).astype(_F32), (0, pad)).reshape(1, LANES)
    dtb_vec = jnp.pad(dt_bias.reshape(-1).astype(_F32), (0, pad)).reshape(1, LANES)
    vec_spec = pl.BlockSpec((1, LANES), lambda i: (0, 0))
    return pl.pallas_call(
        _gate_kernel,
        grid=(m // tm,),
        in_specs=[pl.BlockSpec((tm, LANES), lambda i: (i, 0)), vec_spec, vec_spec],
        out_specs=pl.BlockSpec((tm, LANES), lambda i: (i, 0)),
        out_shape=jax.ShapeDtypeStruct((m, LANES), _F32),
        compiler_params=_params("arbitrary"),
        name="gdn_gates",
    )(ab, alog_vec, dtb_vec)


GDN_CONV_ROWS = 256


def _shift_rows(x, prev_row, next_row):
    rows = x.shape[0]
    row = lax.broadcasted_iota(jnp.int32, x.shape, 0)
    xm1 = jnp.where(row == 0, prev_row, pltpu.roll(x, 1, 0))
    xp1 = jnp.where(row == rows - 1, next_row, pltpu.roll(x, rows - 1, 0))
    return xm1, xp1


def _conv_silu(x_refs, w_refs, dst_refs, l2_scales, rows=GDN_CONV_ROWS):
    seq = x_refs[0].shape[1]
    n_tiles = seq // rows
    ws = [w_ref[...] for w_ref in w_refs]

    def body(r, carry):
        t0 = pl.multiple_of(r * rows, rows)
        p0 = pl.multiple_of(jnp.maximum(t0 - 8, 0), 8)
        n0 = pl.multiple_of(jnp.minimum(t0 + rows, seq - 8), 8)
        for x_ref, w, dst_ref, l2_scale in zip(x_refs, ws, dst_refs, l2_scales):
            x = x_ref[0, pl.ds(t0, rows), :]
            prev_row = jnp.where(r == 0, 0.0, x_ref[0, pl.ds(p0, 8), :][7:8, :])
            next_row = jnp.where(r == n_tiles - 1, 0.0, x_ref[0, pl.ds(n0, 8), :][0:1, :])
            xm1, xp1 = _shift_rows(x, prev_row, next_row)
            y = xm1 * w[0:1, :] + x * w[1:2, :] + xp1 * w[2:3, :]
            y = y * _sigmoid(y)
            if l2_scale is not None:
                y = y * lax.rsqrt(jnp.sum(y * y, axis=-1, keepdims=True) + NORM_EPS) * l2_scale
            dst_ref[pl.ds(t0, rows), :] = y
        return carry

    lax.fori_loop(0, n_tiles, body, 0)


def _unit_tri_inverses(mats):
    c = mats[0].shape[0]
    ri = lax.broadcasted_iota(jnp.int32, (c, 2 * c), 0)
    ci = lax.broadcasted_iota(jnp.int32, (c, 2 * c), 1)
    eye = jnp.where(jnp.logical_or(ri == ci, ri + c == ci), 1.0, 0.0)

    def blockdiag(x):
        return jnp.concatenate([jnp.where(ci < c, x, 0.0), jnp.where(ci >= c, x, 0.0)], axis=0).astype(_BF16)

    xs = [eye - a for a in mats]
    ps = [_dot(a.astype(_BF16), blockdiag(a)) for a in mats]
    yield
    power = 2
    while 2 * power < c:
        xps = [_dot(jnp.concatenate([x, p], axis=0).astype(_BF16), blockdiag(p)) for x, p in zip(xs, ps)]
        xs = [x + xp[:c] for x, xp in zip(xs, xps)]
        ps = [xp[c:] for xp in xps]
        power *= 2
        yield
    return [x + _dot(x.astype(_BF16), blockdiag(p)) for x, p in zip(xs, ps)]


def _interleave(*stages):
    results = [None] * len(stages)
    live = list(range(len(stages)))
    while live:
        for i in list(live):
            try:
                next(stages[i])
            except StopIteration as stop:
                results[i] = stop.value
                live.remove(i)
    return results


GDN_GROUP = 8
GDN_SET_FIELDS = 9


def _gdn_set_shapes():
    g, c, d = GDN_GROUP, GDN_CHUNK, HEAD_DIM
    per_dir = [pltpu.VMEM((g * 4 * c, d), _BF16), pltpu.VMEM((g * d, d), _BF16), pltpu.VMEM((g * c, d), _F32),
               pltpu.VMEM((g, d), _F32)]
    return per_dir * 2 + [pltpu.VMEM((g * c, 2 * c), _BF16)]


def _gdn_prepare_group(cidx_lists, *, head, qn, kn, vn, slab_ref, grow_ref, ring_set):
    c, d = GDN_CHUNK, HEAD_DIM
    ri = lax.broadcasted_iota(jnp.int32, (c, 2 * c), 0)
    ci = lax.broadcasted_iota(jnp.int32, (c, 2 * c), 1)
    left = ci < c
    cil = jnp.where(left, ci, ci - c)
    fwd_i = left.astype(jnp.int32)
    incl = (ri - cil) * (2 * fwd_i - 1) >= 0
    strict = (ri - cil) * (2 * fwd_i - 1) > 0
    lane = lax.broadcasted_iota(jnp.int32, (c, LANES), 1)
    zero_k = jnp.zeros((c, d), _BF16)
    zero_rhs = jnp.zeros((c, 2 * d), _BF16)
    pairs = []
    for slot, (cf, cb) in enumerate(zip(*cidx_lists)):
        halves = []
        for forward, cidx in ((True, cf), (False, cb)):
            r0 = pl.multiple_of(cidx * c, c)
            slab = slab_ref[0, pl.ds(r0, c), :]
            g_lane = head if forward else GDN_HEADS + head
            b_lane = 2 * GDN_HEADS + g_lane
            gcol = jnp.sum(jnp.where(lane == g_lane, slab, 0.0), axis=-1, keepdims=True)
            bcol = jnp.sum(jnp.where(lane == b_lane, slab, 0.0), axis=-1, keepdims=True)
            halves.append(dict(qc=qn[pl.ds(r0, c), :], kc=kn[pl.ds(r0, c), :], vc=vn[pl.ds(r0, c), :],
                               gcol=gcol, bcol=bcol, glast=gcol[c - 1:c, :] if forward else gcol[0:1, :]))
        f, b = halves
        k16f, k16b = f["kc"].astype(_BF16), b["kc"].astype(_BF16)
        lhs = jnp.concatenate([jnp.concatenate([f["qc"], b["qc"]], axis=1),
                               jnp.concatenate([f["kc"], b["kc"]], axis=1)], axis=0).astype(_BF16)
        rhs_t = jnp.concatenate([jnp.concatenate([k16f, zero_k], axis=1),
                                 jnp.concatenate([zero_k, k16b], axis=1)], axis=0)
        prod = _dot_nt(lhs, rhs_t)
        gcol2 = jnp.where(left, f["gcol"], b["gcol"])
        bcol2 = jnp.where(left, f["bcol"], b["bcol"])
        grow2 = grow_ref[0, 0, pl.ds(cf, 1), :]
        decay = jnp.where(incl, jnp.exp(jnp.where(incl, gcol2 - grow2, 0.0)), 0.0)
        pairs.append(dict(slot=slot, f=f, b=b, qk=prod[:c], decay=decay,
                          a=jnp.where(strict, prod[c:] * decay * bcol2, 0.0)))
    yield
    tinvs = yield from _unit_tri_inverses([pr["a"] for pr in pairs])
    yield
    uws = []
    for pr, tinv in zip(pairs, tinvs):
        rhs = []
        for hv in (pr["f"], pr["b"]):
            hv["eg"] = jnp.exp(hv["gcol"])
            rhs.append(jnp.concatenate([hv["vc"] * hv["bcol"], hv["kc"] * hv["bcol"] * hv["eg"]],
                                       axis=1).astype(_BF16))
        rhs2 = jnp.concatenate([jnp.concatenate([rhs[0], zero_rhs], axis=1),
                                jnp.concatenate([zero_rhs, rhs[1]], axis=1)], axis=0)
        uws.append(_dot(tinv.astype(_BF16), rhs2))
    yield
    mcs = []
    for pr, uw in zip(pairs, uws):
        for hv, off in ((pr["f"], 0), (pr["b"], 2 * d)):
            kdt = (hv["kc"] * jnp.exp(hv["glast"] - hv["gcol"])).T.astype(_BF16)
            mcs.append(_dot(kdt, uw[:, off:off + 2 * d].astype(_BF16)))
    yield
    intra_ref = ring_set[-1]
    for i, (pr, uw) in enumerate(zip(pairs, uws)):
        slot = pr["slot"]
        for hv, off, mc, ring in ((pr["f"], 0, mcs[2 * i], ring_set[0:4]), (pr["b"], 2 * d, mcs[2 * i + 1], ring_set[4:8])):
            lhs_ref, c_ref, u_ref, dec_ref = ring
            lhs_ref[pl.ds(slot * 4 * c, 2 * c), :] = (-mc[:, d:]).astype(_BF16)
            lhs_ref[pl.ds(slot * 4 * c + 2 * c, c), :] = uw[:, off + d:off + 2 * d].astype(_BF16)
            lhs_ref[pl.ds(slot * 4 * c + 3 * c, c), :] = (hv["qc"] * hv["eg"]).astype(_BF16)
            c_ref[pl.ds(slot * 2 * c, 2 * c), :] = mc[:, :d].astype(_BF16)
            u_ref[pl.ds(slot * c, c), :] = uw[:, off:off + d]
            dec_ref[pl.ds(slot, 1), :] = jnp.broadcast_to(jnp.exp(hv["glast"]), (1, d))
        intra_ref[pl.ds(slot * c, c), :] = jnp.where(incl, pr["qk"] * pr["decay"], 0.0).astype(_BF16)


def _gdn_scan_group(states, cidx_lists, ring_set, out_refs):
    c = GDN_CHUNK
    rings = (ring_set[0:4], ring_set[4:8])
    intra_ref = ring_set[-1]
    zero_v = jnp.zeros((c, HEAD_DIM), _BF16)
    for slot in range(GDN_GROUP):
        prods = [_dot(ring[0][pl.ds(slot * 4 * c, 4 * c), :], s.astype(_BF16)) for ring, s in zip(rings, states)]
        states = [s * ring[3][pl.ds(slot, 1), :] + p[:2 * c] + ring[1][pl.ds(slot * 2 * c, 2 * c), :].astype(_F32)
                  for ring, s, p in zip(rings, states, prods)]
        intra = intra_ref[pl.ds(slot * c, c), :]
        for forward, ring, cidxs, out_ref, p in zip((True, False), rings, cidx_lists, out_refs, prods):
            v16 = (ring[2][pl.ds(slot * c, c), :] - p[2 * c:3 * c]).astype(_BF16)
            v_pad = jnp.concatenate([v16, zero_v] if forward else [zero_v, v16], axis=0)
            r0 = pl.multiple_of(cidxs[slot] * c, c)
            out_ref[pl.ds(r0, c), :] = p[3 * c:] + _dot(intra, v_pad)
        yield
    return tuple(states)


def _gdn_kernel(q_ref, k_ref, v_ref, cwq_ref, cwk_ref, cwv_ref, slab_ref, grow_ref, z_ref, nw_ref,
                o_ref, qn, kn, vn, of, ob, *ring_refs):
    head = pl.program_id(1)
    seq = q_ref.shape[1]
    n_chunks = seq // GDN_CHUNK
    n_groups = n_chunks // GDN_GROUP
    _conv_silu((q_ref, k_ref, v_ref), (cwq_ref, cwk_ref, cwv_ref), (qn, kn, vn), (HEAD_DIM ** -0.5, 1.0, None))

    nf = GDN_SET_FIELDS
    sets = [tuple(ring_refs[s * nf:(s + 1) * nf]) for s in range(2)]

    def groups(n):
        return ([n * GDN_GROUP + j for j in range(GDN_GROUP)],
                [n_chunks - 1 - n * GDN_GROUP - j for j in range(GDN_GROUP)])

    def prepare(n, ring_set):
        return _gdn_prepare_group(groups(n), head=head, qn=qn, kn=kn, vn=vn, slab_ref=slab_ref,
                                  grow_ref=grow_ref, ring_set=ring_set)

    def scan(n, states, ring_set):
        return _gdn_scan_group(states, groups(n), ring_set, (of, ob))

    def stage(n, states, scan_set, prep_set):
        return _interleave(scan(n, states, scan_set), prepare(n + 1, prep_set))[0]

    def stage_pair(m, states):
        n = 2 * m
        states = stage(n, states, sets[0], sets[1])
        return stage(n + 1, states, sets[1], sets[0])

    zero = jnp.zeros((HEAD_DIM, HEAD_DIM), _F32)
    _interleave(prepare(0, sets[0]))
    states = lax.fori_loop(0, n_groups // 2 - 1, stage_pair, (zero, zero))
    states = stage(n_groups - 2, states, sets[0], sets[1])
    _interleave(scan(n_groups - 1, states, sets[1]))

    rows = 256

    def finish(r, carry):
        sl = pl.ds(pl.multiple_of(r * rows, rows), rows)
        o = of[sl, :] + ob[sl, :]
        o = o * lax.rsqrt(jnp.mean(o * o, axis=-1, keepdims=True) + NORM_EPS) * nw_ref[...]
        z = z_ref[0, sl, :].astype(_F32)
        o_ref[0, sl, :] = (o * (z * _sigmoid(z))).astype(o_ref.dtype)
        return carry

    lax.fori_loop(0, seq // rows, finish, 0)


def _gdn_mixer(qkv, z, slab, conv_w, norm_w):
    bsz, seq, _ = qkv.shape
    c = GDN_CHUNK
    nc = seq // c
    h = GDN_HEADS
    slab_t = slab.transpose(0, 2, 1).reshape(bsz, LANES, nc, c)
    grow = jnp.concatenate([slab_t[:, :h], slab_t[:, h:2 * h, ::-1]], axis=-1)
    col = lambda off: pl.BlockSpec((1, seq, HEAD_DIM), functools.partial(lambda b, hh, o: (b, 0, o + hh), o=off))
    cw = lambda off: pl.BlockSpec((3, HEAD_DIM), functools.partial(lambda b, hh, o: (0, o + hh), o=off))
    return pl.pallas_call(
        _gdn_kernel,
        grid=(bsz, h),
        in_specs=[col(0), col(h), col(2 * h), cw(0), cw(h), cw(2 * h),
                  pl.BlockSpec((1, seq, LANES), lambda b, hh: (b, 0, 0)),
                  pl.BlockSpec((1, 1, nc, 2 * c), lambda b, hh: (b, hh, 0, 0)),
                  col(0),
                  pl.BlockSpec((1, HEAD_DIM), lambda b, hh: (0, 0))],
        out_specs=col(0),
        out_shape=jax.ShapeDtypeStruct((bsz, seq, GDN_WIDTH), _BF16),
        scratch_shapes=([pltpu.VMEM((seq, HEAD_DIM), _F32)] * 5
                        + _gdn_set_shapes() * 2),
        compiler_params=_params("arbitrary", "arbitrary"),
        name="gdn_mixer",
    )(qkv, qkv, qkv, conv_w, conv_w, conv_w, slab, grow, z, norm_w.reshape(1, HEAD_DIM).astype(_F32))


ATT_RADIUS = 64
ATT_BQ = 128
ATT_WIDTH_KEYS = ATT_BQ + 2 * ATT_RADIUS
ATT_GROUP = 8
ATT_SPLIT = 4
assert all(w // (2 * d) == ATT_RADIUS for w, d in DILATED_PATTERNS)
assert tuple(d for _, d in DILATED_PATTERNS) == (1, ATT_SPLIT, ATT_SPLIT * ATT_SPLIT)


def _attn_blocks(qs, kws, vws, biases):
    scale = HEAD_DIM ** -0.5
    ss = [_dot_nt(q, kw) * scale + b for q, kw, b in zip(qs, kws, biases)]
    ms = [jnp.max(s, axis=-1, keepdims=True) for s in ss]
    es = [jnp.exp(s - m) for s, m in zip(ss, ms)]
    ls = [jnp.sum(e, axis=-1, keepdims=True) for e in es]
    accs = [_dot(e.astype(_BF16), vw) for e, vw in zip(es, vws)]
    return accs, ms, ls


def _attn_merge(acc_a, m_a, l_a, acc_b, m_b, l_b):
    m = jnp.maximum(m_a, m_b)
    wa = jnp.exp(m_a - m)
    wb = jnp.exp(m_b - m)
    return wa * acc_a + wb * acc_b, m, wa * l_a + wb * l_b


def _attn_kernel(slope_ref, q_ref, k_ref, v_ref, o_ref, nat, xq, xk, xv, acc4, m4, l4, acc1, m1, l1, bias_scr):
    seq = q_ref.shape[1]
    bq, width, radius, split, group = ATT_BQ, ATT_WIDTH_KEYS, ATT_RADIUS, ATT_SPLIT, ATT_GROUP
    n4 = seq // split
    n16 = n4 // split
    rep = lambda x: jnp.broadcast_to(x, (bq, HEAD_DIM))

    slope = slope_ref[0][:, 0:1]
    rel = (lax.broadcasted_iota(jnp.int32, (bq, width), 1) - lax.broadcasted_iota(jnp.int32, (bq, width), 0))
    for p, (_, dilation) in enumerate(DILATED_PATTERNS):
        for pos in range(3):
            dist = jnp.abs(rel - pos * radius)
            bias_scr[p, pos] = jnp.where(dist <= radius, -(slope * float(dilation)) * dist.astype(_F32), NEG_INF)

    def window(i0, n):
        ws = pl.multiple_of(jnp.clip(i0 - radius, 0, n - width), radius)
        return ws, (i0 - ws) // radius

    rows = 512
    for src, dst in ((q_ref, xq), (k_ref, xk), (v_ref, xv)):
        def widen(r, carry, src=src):
            sl = pl.ds(pl.multiple_of(r * rows, rows), rows)
            nat[sl, :] = src[0, sl, :].astype(_F32)
            return carry
        lax.fori_loop(0, seq // rows, widen, 0)
        for cls in range(split):
            def gather(r, carry, dst=dst, cls=cls):
                r0 = pl.multiple_of(r * rows, rows)
                dst[cls, pl.ds(r0, rows), :] = nat[pl.ds(cls + split * r0, rows, stride=split), :]
                return carry
            lax.fori_loop(0, n4 // rows, gather, 0)

    per_step = group // split

    def dil4(i, carry):
        qs, kws, vws, biases, q_rows = [], [], [], [], []
        for j in range(per_step):
            i0 = pl.multiple_of((i * per_step + j) * bq, bq)
            ws, pos = window(i0, n4)
            for c in range(split):
                q_rows.append((c, pl.ds(i0, bq)))
                qs.append(xq[c, pl.ds(i0, bq), :].astype(_BF16))
                kws.append(xk[c, pl.ds(ws, width), :].astype(_BF16))
                vws.append(xv[c, pl.ds(ws, width), :].astype(_BF16))
                biases.append(bias_scr[1, pos])
        accs, ms, ls = _attn_blocks(qs, kws, vws, biases)
        for (c, rows_), acc, m, l in zip(q_rows, accs, ms, ls):
            acc4[c, rows_, :] = acc
            m4[c, rows_, :] = rep(m)
            l4[c, rows_, :] = rep(l)
        return carry

    lax.fori_loop(0, n4 // (bq * per_step), dil4, 0)

    def dil16(i, carry):
        qs, kws, vws, biases, q_rows = [], [], [], [], []
        for j in range(per_step):
            blk = i * per_step + j
            sub = blk // (n16 // bq)
            i0 = pl.multiple_of((blk % (n16 // bq)) * bq, bq)
            ws, pos = window(i0, n16)
            k_rows = pl.ds(sub + split * ws, width, stride=split)
            for c in range(split):
                q_rows.append((c, pl.ds(sub + split * i0, bq, stride=split)))
                qs.append(xq[c, q_rows[-1][1], :].astype(_BF16))
                kws.append(xk[c, k_rows, :].astype(_BF16))
                vws.append(xv[c, k_rows, :].astype(_BF16))
                biases.append(bias_scr[2, pos])
        accs, ms, ls = _attn_blocks(qs, kws, vws, biases)
        for (c, rows_), acc_b, m_b, l_b in zip(q_rows, accs, ms, ls):
            acc, m, l = _attn_merge(acc4[c, rows_, :], m4[c, rows_, :], l4[c, rows_, :], acc_b, m_b, l_b)
            acc4[c, rows_, :] = acc
            m4[c, rows_, :] = m
            l4[c, rows_, :] = l
        return carry

    lax.fori_loop(0, split * (n16 // bq) // per_step, dil16, 0)

    def dil1(i, carry):
        qs, kws, vws, biases, i0s = [], [], [], [], []
        for j in range(group):
            i0 = pl.multiple_of((i * group + j) * bq, bq)
            ws, pos = window(i0, seq)
            i0s.append(i0)
            qs.append(q_ref[0, pl.ds(i0, bq), :])
            kws.append(k_ref[0, pl.ds(ws, width), :])
            vws.append(v_ref[0, pl.ds(ws, width), :])
            biases.append(bias_scr[0, pos])
        accs, ms, ls = _attn_blocks(qs, kws, vws, biases)
        for j in range(group):
            acc1[pl.ds(i0s[j], bq), :] = accs[j]
            m1[pl.ds(i0s[j], bq), :] = rep(ms[j])
            l1[pl.ds(i0s[j], bq), :] = rep(ls[j])
        return carry

    lax.fori_loop(0, seq // (bq * group), dil1, 0)

    for cls in range(split):
        def finish(r, carry, cls=cls):
            r0 = pl.multiple_of(r * bq, bq)
            nat_rows = pl.ds(cls + split * r0, bq, stride=split)
            acc, _, l = _attn_merge(acc1[nat_rows, :], m1[nat_rows, :], l1[nat_rows, :],
                                    acc4[cls, pl.ds(r0, bq), :], m4[cls, pl.ds(r0, bq), :], l4[cls, pl.ds(r0, bq), :])
            nat[nat_rows, :] = acc / l
            return carry
        lax.fori_loop(0, n4 // bq, finish, 0)

    def narrow(r, carry):
        sl = pl.ds(pl.multiple_of(r * rows, rows), rows)
        o_ref[0, sl, :] = nat[sl, :].astype(o_ref.dtype)
        return carry

    lax.fori_loop(0, seq // rows, narrow, 0)


def _dilated_attention(qkv, slopes):
    bsz, seq, _ = qkv.shape
    h = ATT_HEADS
    n4 = seq // ATT_SPLIT
    n16 = n4 // ATT_SPLIT
    assert n16 % ATT_BQ == 0 and n16 >= ATT_WIDTH_KEYS and seq % (ATT_BQ * ATT_GROUP) == 0
    assert ATT_GROUP % ATT_SPLIT == 0 and (ATT_SPLIT * n16 // ATT_BQ) % (ATT_GROUP // ATT_SPLIT) == 0
    col = lambda off: pl.BlockSpec((1, seq, HEAD_DIM), functools.partial(lambda b, hh, o: (b, 0, o + hh), o=off))
    slab = pltpu.VMEM((ATT_SPLIT, n4, HEAD_DIM), _F32)
    nat = pltpu.VMEM((seq, HEAD_DIM), _F32)
    return pl.pallas_call(
        _attn_kernel,
        grid=(bsz, h),
        in_specs=[pl.BlockSpec((1, 1, LANES), lambda b, hh: (hh, 0, 0)), col(0), col(h), col(2 * h)],
        out_specs=col(0),
        out_shape=jax.ShapeDtypeStruct((bsz, seq, ATT_WIDTH), _BF16),
        scratch_shapes=[nat, slab, slab, slab, slab, slab, slab, nat, nat, nat,
                        pltpu.VMEM((len(DILATED_PATTERNS), 3, ATT_BQ, ATT_WIDTH_KEYS), _F32)],
        compiler_params=_params("arbitrary", "arbitrary"),
        name="dilated_attention",
    )(slopes, qkv, qkv, qkv)


FFN_EPILOGUE_ROWS = 32

def _ffn_up_kernel(a_ref, wg_ref, wu_ref, cg_ref, cu_ref, wd_ref, o_ref, wd16_ref, wcat, y_even, y_odd, *,
                   n_tiles, tiles_per_seq, halo):
    s = pl.program_id(0)
    n_steps = pl.num_programs(0) - 1
    tn = wg_ref.shape[1]
    tm = o_ref.shape[0]
    wd16_ref[...] = wd_ref[...].astype(wd16_ref.dtype)

    @pl.when(jnp.logical_and(s % n_tiles == 0, s < n_steps))
    def _():
        _cast_rows(wg_ref, wcat.at[:, pl.ds(0, tn)])
        _cast_rows(wu_ref, wcat.at[:, pl.ds(tn, tn)])

    @pl.when(s == 0)
    def _():
        y_odd[...] = jnp.zeros(y_odd.shape, y_odd.dtype)

    tile = jnp.maximum(s - 1, 0) % n_tiles
    first = (tile % tiles_per_seq) == 0
    last = (tile % tiles_per_seq) == tiles_per_seq - 1

    def step(y_new, y_old):
        cw = jnp.concatenate([cg_ref[...], cu_ref[...]], axis=1)
        rows = FFN_EPILOGUE_ROWS
        n_chunks = tm // rows

        def finish(c):
            r0 = halo + c * rows
            y = y_old[pl.ds(r0, rows), :]
            prev_row = y_old[pl.ds(r0 - 8, 8), :][7:8, :]
            next_row = y_old[pl.ds(r0 + rows, 8), :][0:1, :]
            if c == 0:
                prev_row = jnp.where(first, 0.0, prev_row)
            if c == n_chunks - 1:
                next_row = jnp.where(last, 0.0, next_row)
            ym1, yp1 = _shift_rows(y, prev_row, next_row)
            conv = ym1 * cw[0:1, :] + y * cw[1:2, :] + yp1 * cw[2:3, :]
            gate = conv[:, :tn]
            up = conv[:, tn:]
            o_ref[pl.ds(c * rows, rows), :] = (gate * _sigmoid(gate) * up).astype(o_ref.dtype)

        proj = _dot(a_ref[...], wcat[...])
        for c in range(n_chunks):
            finish(c)
        y_new[...] = proj

    @pl.when(s % 2 == 0)
    def _():
        step(y_even, y_odd)

    @pl.when(s % 2 == 1)
    def _():
        step(y_odd, y_even)


def _ffn_up(u_pad, w_up, conv_w, w_down, seq, pad_rows, tm=1024, tn=256):
    m, k = u_pad.shape[0] - 2 * pad_rows, u_pad.shape[1]
    ffn = w_up.shape[1] // 2
    halo = BF16_ROWS
    assert ffn % tn == 0 and seq % tm == 0 and pad_rows >= halo
    nj = ffn // tn
    n_tiles = m // tm
    n_steps = nj * n_tiles
    wd_rows = w_down.shape[0] // n_steps
    assert wd_rows * n_steps == w_down.shape[0] and wd_rows % BF16_ROWS == 0
    wd_spec = pl.BlockSpec((wd_rows, w_down.shape[1]), lambda s: (jnp.minimum(s, n_steps - 1), 0))
    assert tm % halo == 0 and pad_rows % halo == 0
    col = lambda s: jnp.minimum(s // n_tiles, nj - 1)
    col_prev = lambda s: jnp.maximum(s - 1, 0) // n_tiles
    a_spec = pl.BlockSpec((pl.Element(tm + 2 * halo), pl.Element(k)),
                          lambda s: (((s % n_tiles) * (tm // halo) + (pad_rows // halo - 1)) * halo, 0))
    return pl.pallas_call(
        functools.partial(_ffn_up_kernel, n_tiles=n_tiles, tiles_per_seq=seq // tm, halo=halo),
        grid=(n_steps + 1,),
        in_specs=[a_spec,
                  pl.BlockSpec((k, tn), lambda s: (0, col(s))),
                  pl.BlockSpec((k, tn), lambda s: (0, col(s) + nj)),
                  pl.BlockSpec((3, tn), lambda s: (0, col_prev(s))),
                  pl.BlockSpec((3, tn), lambda s: (0, col_prev(s) + nj)),
                  wd_spec],
        out_specs=[pl.BlockSpec((tm, tn), lambda s: (jnp.maximum(s - 1, 0) % n_tiles, col_prev(s))), wd_spec],
        out_shape=[jax.ShapeDtypeStruct((m, ffn), _BF16), jax.ShapeDtypeStruct(w_down.shape, _BF16)],
        scratch_shapes=[pltpu.VMEM((k, 2 * tn), _BF16)] + [pltpu.VMEM((tm + 2 * halo, 2 * tn), _F32)] * 2,
        compiler_params=_params("arbitrary"),
        name="ffn_up_conv_glu",
    )(u_pad, w_up, w_up, conv_w, conv_w, w_down)


def _ple_kernel(a_ref, wg_ref, p_ref, wp_ref, h_ref, o_ref, wgb, wpb):
    @pl.when(pl.program_id(1) == 0)
    def _():
        _cast_rows(wg_ref, wgb)
        _cast_rows(wp_ref, wpb)

    gate = _sigmoid(_dot(a_ref[...], wgb[...]))
    emb = _dot(p_ref[...], wpb[...])
    o_ref[...] = h_ref[...] + gate * emb


def _ple(u, w_gate, p, w_proj, h, *, tm, tn):
    m, k = u.shape
    n = w_gate.shape[1]
    kp = p.shape[1]
    return pl.pallas_call(
        _ple_kernel,
        grid=(n // tn, m // tm),
        in_specs=[pl.BlockSpec((tm, k), lambda j, i: (i, 0)),
                  pl.BlockSpec((k, tn), lambda j, i: (0, j)),
                  pl.BlockSpec((tm, kp), lambda j, i: (i, 0)),
                  pl.BlockSpec((kp, tn), lambda j, i: (0, j)),
                  pl.BlockSpec((tm, tn), lambda j, i: (i, j))],
        out_specs=pl.BlockSpec((tm, tn), lambda j, i: (i, j)),
        out_shape=jax.ShapeDtypeStruct((m, n), _F32),
        scratch_shapes=[pltpu.VMEM((k, tn), _BF16), pltpu.VMEM((kp, tn), _BF16)],
        compiler_params=_params("arbitrary", "arbitrary"),
        name="ple_gate",
    )(u, w_gate, p, w_proj, h)


def kernel(x, p, attn_norm, w_in, gdn_conv, gdn_a_log, gdn_dt_bias, gdn_out_norm, w_out, ffn_norm, w_up,
           ffn_conv, w_down, ple_norm, w_ple_gate, w_ple_proj, final_norm):
    bsz, seq, d_model = x.shape
    depth = w_in.shape[0]
    m = bsz * seq
    qkv_cols = 3 * GDN_WIDTH
    ab_off = qkv_cols + GDN_WIDTH
    att_off = ab_off + 4 * GDN_HEADS
    slopes = jnp.exp2(-ALIBI_MAX_BIAS * (jnp.arange(ATT_HEADS, dtype=_F32) + 1.0) / ATT_HEADS)
    slopes = jnp.broadcast_to(slopes[:, None, None], (ATT_HEADS, 1, LANES))

    h = x.reshape(m, d_model)
    for i in range(depth):
        u = _rmsnorm(h, attn_norm[i], _BF16)
        wt = jnp.swapaxes(w_in[i], 0, 1)
        tm, tn = MM_TM, MM_TN
        gdn_qkv = _matmul_nt(u, wt, n_cols=qkv_cols, row_off=0, tm=tm, tn=tn, out_dtype=_F32, name="proj_gdn_qkv")
        gdn_z = _matmul_nt(u, wt, n_cols=GDN_WIDTH, row_off=qkv_cols, tm=tm, tn=tn, out_dtype=_BF16,
                           name="proj_gdn_z")
        ab = _matmul_nt(u, wt, n_cols=LANES, row_off=ab_off, tm=tm, tn=LANES, out_dtype=_F32, name="proj_gdn_ab")
        att_qkv = _matmul_nt(u, wt, n_cols=3 * ATT_WIDTH, row_off=att_off, tm=tm, tn=tn, out_dtype=_BF16,
                             name="proj_att_qkv")

        slab = _gdn_gates(ab, gdn_a_log[i], gdn_dt_bias[i]).reshape(bsz, seq, LANES)
        o_gdn = _gdn_mixer(gdn_qkv.reshape(bsz, seq, qkv_cols), gdn_z.reshape(bsz, seq, GDN_WIDTH), slab,
                           gdn_conv[i], gdn_out_norm[i]).reshape(m, GDN_WIDTH)

        att3 = att_qkv.reshape(bsz, seq, 3 * ATT_WIDTH)
        o_att = _dilated_attention(att3, slopes).reshape(m, ATT_WIDTH)

        h = _matmul([o_gdn, o_att], w_out[i], n_cols=d_model, tm=tm, tn=tn, out_dtype=_F32, residual=h,
                    name="out_proj")

        norm_tm = 256
        u_pad = _rmsnorm(h, ffn_norm[i], _BF16, tm=norm_tm, pad_tiles=1)
        act, w_down16 = _ffn_up(u_pad, w_up[i], ffn_conv[i], w_down[i], seq, pad_rows=norm_tm)
        h = _matmul([act], w_down16, n_cols=d_model, tm=MM_TM_LONG_K, tn=tn, out_dtype=_F32,
                    residual=h, name="ffn_down")

        u = _rmsnorm(h, ple_norm[i], _BF16)
        h = _ple(u, w_ple_gate[i], p[i].reshape(m, -1).astype(_BF16), w_ple_proj[i], h, tm=tm, tn=tn)
    return _rmsnorm(h, final_norm, x.dtype).reshape(bsz, seq, d_model)
```

```python
import functools

import jax
import jax.numpy as jnp
from jax import lax
from jax.experimental import pallas as pl
from jax.experimental.pallas import tpu as pltpu

HEAD_DIM = 128
GDN_HEADS = 16
ATT_HEADS = 16
GDN_WIDTH = GDN_HEADS * HEAD_DIM
ATT_WIDTH = ATT_HEADS * HEAD_DIM
GDN_CHUNK = 64
DILATED_PATTERNS = ((128, 1), (512, 4), (2048, 16))
ALIBI_MAX_BIAS = 8.0
NORM_EPS = 1e-6
NEG_INF = -1e30

LANES = 128
SUBLANES = 8
BF16_ROWS = 16
VMEM_LIMIT_BYTES = 56 * 1024 * 1024
MM_TM = 1024
MM_TN = 512
MM_TM_LONG_K = 512
NORM_TM = 512

_F32 = jnp.float32
_BF16 = jnp.bfloat16


def _params(*semantics):
    return pltpu.CompilerParams(dimension_semantics=semantics, vmem_limit_bytes=VMEM_LIMIT_BYTES)


def _sigmoid(x):
    return 1.0 / (1.0 + jnp.exp(-x))


def _dot(a, b):
    return jnp.dot(a, b, preferred_element_type=_F32)


def _dot_nt(a, b):
    return lax.dot_general(a, b, (((1,), (1,)), ((), ())), preferred_element_type=_F32)


def _rmsnorm_kernel(x_ref, w_ref, o_ref, *, pad_tiles):
    x = x_ref[...].astype(_F32)
    ms = jnp.mean(x * x, axis=-1, keepdims=True)
    y = x * lax.rsqrt(ms + NORM_EPS) * w_ref[...]
    if pad_tiles:
        i = pl.program_id(0)
        inside = jnp.logical_and(i >= pad_tiles, i < pl.num_programs(0) - pad_tiles)
        y = jnp.where(inside, y, 0.0)
    o_ref[...] = y.astype(o_ref.dtype)


def _rmsnorm(x, w, out_dtype, tm=NORM_TM, pad_tiles=0):
    m, d = x.shape
    nt = m // tm
    return pl.pallas_call(
        functools.partial(_rmsnorm_kernel, pad_tiles=pad_tiles),
        grid=(nt + 2 * pad_tiles,),
        in_specs=[pl.BlockSpec((tm, d), lambda i: (jnp.clip(i - pad_tiles, 0, nt - 1), 0)),
                  pl.BlockSpec((1, d), lambda i: (0, 0))],
        out_specs=pl.BlockSpec((tm, d), lambda i: (i, 0)),
        out_shape=jax.ShapeDtypeStruct((m + 2 * pad_tiles * tm, d), out_dtype),
        compiler_params=_params("arbitrary"),
        name="rmsnorm",
    )(x, w.reshape(1, d).astype(_F32))


def _cast_rows(src_ref, dst_ref, rows=256):
    def body(r, carry):
        sl = pl.ds(pl.multiple_of(r * rows, rows), rows)
        dst_ref[sl, :] = src_ref[sl, :].astype(dst_ref.dtype)
        return carry
    lax.fori_loop(0, src_ref.shape[0] // rows, body, 0)


def _mm_kernel(*refs, n_a, has_res, cast_w):
    a_refs = refs[:n_a]
    w_refs = refs[n_a:2 * n_a]
    pos = 2 * n_a
    res_ref = refs[pos] if has_res else None
    pos += int(has_res)
    o_ref = refs[pos]
    wb_refs = refs[pos + 1:pos + 1 + n_a] if cast_w else w_refs
    if cast_w:
        @pl.when(pl.program_id(1) == 0)
        def _():
            for w_ref, wb_ref in zip(w_refs, wb_refs):
                _cast_rows(w_ref, wb_ref)
    acc = None
    for a_ref, wb_ref in zip(a_refs, wb_refs):
        part = _dot(a_ref[...], wb_ref[...])
        acc = part if acc is None else acc + part
    if has_res:
        acc = acc + res_ref[...]
    o_ref[...] = acc.astype(o_ref.dtype)


def _mm_nt_kernel(a_ref, wt_ref, o_ref, wb_ref):
    @pl.when(pl.program_id(1) == 0)
    def _():
        _cast_rows(wt_ref, wb_ref, rows=SUBLANES * 8)

    o_ref[...] = _dot_nt(a_ref[...], wb_ref[...]).astype(o_ref.dtype)


def _matmul_nt(a, wt, *, n_cols, row_off, tm, tn, out_dtype, name):
    m, k = a.shape
    assert n_cols % tn == 0 and m % tm == 0 and row_off % SUBLANES == 0 and tn % SUBLANES == 0
    return pl.pallas_call(
        _mm_nt_kernel,
        grid=(n_cols // tn, m // tm),
        in_specs=[pl.BlockSpec((tm, k), lambda j, i: (i, 0)),
                  pl.BlockSpec((pl.Element(tn), pl.Element(k)),
                               lambda j, i: ((j * (tn // SUBLANES) + row_off // SUBLANES) * SUBLANES, 0))],
        out_specs=pl.BlockSpec((tm, tn), lambda j, i: (i, j)),
        out_shape=jax.ShapeDtypeStruct((m, n_cols), out_dtype),
        scratch_shapes=[pltpu.VMEM((tn, k), _BF16)],
        compiler_params=_params("arbitrary", "arbitrary"),
        name=name,
    )(a, wt)


def _matmul(a_list, w, *, n_cols, col_off=0, tm, tn, out_dtype, residual=None, name):
    m = a_list[0].shape[0]
    assert col_off % tn == 0 and n_cols % tn == 0 and m % tm == 0
    off = col_off // tn
    cast_w = w.dtype != _BF16
    in_specs, scratch = [], []
    for a in a_list:
        in_specs.append(pl.BlockSpec((tm, a.shape[1]), lambda j, i: (i, 0)))
    row_blk = 0
    for a in a_list:
        kk = a.shape[1]
        assert all(b.shape[1] == kk for b in a_list)
        in_specs.append(pl.BlockSpec((kk, tn), functools.partial(lambda j, i, rb: (rb, j + off), rb=row_blk)))
        if cast_w:
            scratch.append(pltpu.VMEM((kk, tn), _BF16))
        row_blk += 1
    args = list(a_list) + [w] * len(a_list)
    if residual is not None:
        in_specs.append(pl.BlockSpec((tm, tn), lambda j, i: (i, j)))
        args.append(residual)
    return pl.pallas_call(
        functools.partial(_mm_kernel, n_a=len(a_list), has_res=residual is not None, cast_w=cast_w),
        grid=(n_cols // tn, m // tm),
        in_specs=in_specs,
        out_specs=pl.BlockSpec((tm, tn), lambda j, i: (i, j)),
        out_shape=jax.ShapeDtypeStruct((m, n_cols), out_dtype),
        scratch_shapes=scratch,
        compiler_params=_params("arbitrary", "arbitrary"),
        name=name,
    )(*args)


def _gate_kernel(ab_ref, alog_ref, dtb_ref, o_ref):
    x = ab_ref[...]
    rows = x.shape[0]
    xa = x + dtb_ref[...]
    softplus = jnp.maximum(xa, 0.0) + jnp.log(1.0 + jnp.exp(-jnp.abs(xa)))
    g = -jnp.exp(alog_ref[...]) * softplus
    beta = _sigmoid(x)
    c = GDN_CHUNK
    ri = lax.broadcasted_iota(jnp.int32, (c, c), 0)
    ci = lax.broadcasted_iota(jnp.int32, (c, c), 1)
    lower = (ri >= ci).astype(_F32)
    upper = (ri <= ci).astype(_F32)
    lane = lax.broadcasted_iota(jnp.int32, (c, LANES), 1)
    for n in range(rows // c):
        gc = g[n * c:(n + 1) * c]
        prefix = jnp.dot(lower, gc, precision=lax.Precision.HIGHEST, preferred_element_type=_F32)
        suffix = jnp.dot(upper, gc, precision=lax.Precision.HIGHEST, preferred_element_type=_F32)
        out = jnp.where(lane < GDN_HEADS, prefix,
                        jnp.where(lane < 2 * GDN_HEADS, suffix,
                                  jnp.where(lane < 4 * GDN_HEADS, beta[n * c:(n + 1) * c], 0.0)))
        o_ref[n * c:(n + 1) * c, :] = out


def _gdn_gates(ab, a_log, dt_bias, tm=512):
    m = ab.shape[0]
    pad = LANES - 2 * GDN_HEADS
    alog_vec = jnp.pad(a_log.reshape(-1).astype(_F32), (0, pad)).reshape(1, LANES)
    dtb_vec = jnp.pad(dt_bias.reshape(-1).astype(_F32), (0, pad)).reshape(1, LANES)
    vec_spec = pl.BlockSpec((1, LANES), lambda i: (0, 0))
    return pl.pallas_call(
        _gate_kernel,
        grid=(m // tm,),
        in_specs=[pl.BlockSpec((tm, LANES), lambda i: (i, 0)), vec_spec, vec_spec],
        out_specs=pl.BlockSpec((tm, LANES), lambda i: (i, 0)),
        out_shape=jax.ShapeDtypeStruct((m, LANES), _F32),
        compiler_params=_params("arbitrary"),
        name="gdn_gates",
    )(ab, alog_vec, dtb_vec)


GDN_CONV_ROWS = 256


def _shift_rows(x, prev_row, next_row):
    rows = x.shape[0]
    row = lax.broadcasted_iota(jnp.int32, x.shape, 0)
    xm1 = jnp.where(row == 0, prev_row, pltpu.roll(x, 1, 0))
    xp1 = jnp.where(row == rows - 1, next_row, pltpu.roll(x, rows - 1, 0))
    return xm1, xp1


def _conv_silu_tiles(starts, x_refs, w_refs, dst_refs, l2_scales, rows=GDN_CONV_ROWS):
    seq = x_refs[0].shape[1]
    ws = [w_ref[...] for w_ref in w_refs]
    for start in starts:
        t0 = pl.multiple_of(start, rows)
        p0 = pl.multiple_of(jnp.maximum(t0 - 8, 0), 8)
        n0 = pl.multiple_of(jnp.minimum(t0 + rows, seq - 8), 8)
        for x_ref, w, dst_ref, l2_scale in zip(x_refs, ws, dst_refs, l2_scales):
            x = x_ref[0, pl.ds(t0, rows), :]
            prev_row = jnp.where(t0 == 0, 0.0, x_ref[0, pl.ds(p0, 8), :][7:8, :])
            next_row = jnp.where(t0 == seq - rows, 0.0, x_ref[0, pl.ds(n0, 8), :][0:1, :])
            xm1, xp1 = _shift_rows(x, prev_row, next_row)
            y = xm1 * w[0:1, :] + x * w[1:2, :] + xp1 * w[2:3, :]
            y = y * _sigmoid(y)
            if l2_scale is not None:
                y = y * lax.rsqrt(jnp.sum(y * y, axis=-1, keepdims=True) + NORM_EPS) * l2_scale
            dst_ref[pl.ds(t0, rows), :] = y
        yield


def _unit_tri_inverses(mats):
    c = mats[0].shape[0]
    ri = lax.broadcasted_iota(jnp.int32, (c, 2 * c), 0)
    ci = lax.broadcasted_iota(jnp.int32, (c, 2 * c), 1)
    eye = jnp.where(jnp.logical_or(ri == ci, ri + c == ci), 1.0, 0.0)

    def blockdiag(x):
        return jnp.concatenate([jnp.where(ci < c, x, 0.0), jnp.where(ci >= c, x, 0.0)], axis=0).astype(_BF16)

    xs = [eye - a for a in mats]
    ps = [_dot(a.astype(_BF16), blockdiag(a)) for a in mats]
    yield
    power = 2
    while 2 * power < c:
        xps = [_dot(jnp.concatenate([x, p], axis=0).astype(_BF16), blockdiag(p)) for x, p in zip(xs, ps)]
        xs = [x + xp[:c] for x, xp in zip(xs, xps)]
        ps = [xp[c:] for xp in xps]
        power *= 2
        yield
    return [x + _dot(x.astype(_BF16), blockdiag(p)) for x, p in zip(xs, ps)]


def _interleave(*stages):
    results = [None] * len(stages)
    live = list(range(len(stages)))
    while live:
        for i in list(live):
            try:
                next(stages[i])
            except StopIteration as stop:
                results[i] = stop.value
                live.remove(i)
    return results


GDN_GROUP = 8
GDN_SET_FIELDS = 9


def _gdn_set_shapes():
    g, c, d = GDN_GROUP, GDN_CHUNK, HEAD_DIM
    per_dir = [pltpu.VMEM((g * 4 * c, d), _BF16), pltpu.VMEM((g * d, d), _BF16), pltpu.VMEM((g * c, d), _F32),
               pltpu.VMEM((g, d), _F32)]
    return per_dir * 2 + [pltpu.VMEM((g * c, 2 * c), _BF16)]


def _gdn_prepare_group(cidx_lists, *, head, qn, kn, vn, slab_ref, grow_ref, ring_set):
    c, d = GDN_CHUNK, HEAD_DIM
    ri = lax.broadcasted_iota(jnp.int32, (c, 2 * c), 0)
    ci = lax.broadcasted_iota(jnp.int32, (c, 2 * c), 1)
    left = ci < c
    cil = jnp.where(left, ci, ci - c)
    fwd_i = left.astype(jnp.int32)
    incl = (ri - cil) * (2 * fwd_i - 1) >= 0
    strict = (ri - cil) * (2 * fwd_i - 1) > 0
    lane = lax.broadcasted_iota(jnp.int32, (c, LANES), 1)
    zero_k = jnp.zeros((c, d), _BF16)
    zero_rhs = jnp.zeros((c, 2 * d), _BF16)
    pairs = []
    for slot, (cf, cb) in enumerate(zip(*cidx_lists)):
        halves = []
        for forward, cidx in ((True, cf), (False, cb)):
            r0 = pl.multiple_of(cidx * c, c)
            slab = slab_ref[0, pl.ds(r0, c), :]
            g_lane = head if forward else GDN_HEADS + head
            b_lane = 2 * GDN_HEADS + g_lane
            gcol = jnp.sum(jnp.where(lane == g_lane, slab, 0.0), axis=-1, keepdims=True)
            bcol = jnp.sum(jnp.where(lane == b_lane, slab, 0.0), axis=-1, keepdims=True)
            halves.append(dict(qc=qn[pl.ds(r0, c), :], kc=kn[pl.ds(r0, c), :], vc=vn[pl.ds(r0, c), :],
                               gcol=gcol, bcol=bcol, glast=gcol[c - 1:c, :] if forward else gcol[0:1, :]))
        f, b = halves
        k16f, k16b = f["kc"].astype(_BF16), b["kc"].astype(_BF16)
        lhs = jnp.concatenate([jnp.concatenate([f["qc"], b["qc"]], axis=1),
                               jnp.concatenate([f["kc"], b["kc"]], axis=1)], axis=0).astype(_BF16)
        rhs_t = jnp.concatenate([jnp.concatenate([k16f, zero_k], axis=1),
                                 jnp.concatenate([zero_k, k16b], axis=1)], axis=0)
        prod = _dot_nt(lhs, rhs_t)
        gcol2 = jnp.where(left, f["gcol"], b["gcol"])
        bcol2 = jnp.where(left, f["bcol"], b["bcol"])
        grow2 = grow_ref[0, 0, pl.ds(cf, 1), :]
        decay = jnp.where(incl, jnp.exp(jnp.where(incl, gcol2 - grow2, 0.0)), 0.0)
        pairs.append(dict(slot=slot, f=f, b=b, qk=prod[:c], decay=decay,
                          a=jnp.where(strict, prod[c:] * decay * bcol2, 0.0)))
    yield
    tinvs = yield from _unit_tri_inverses([pr["a"] for pr in pairs])
    yield
    uws = []
    for pr, tinv in zip(pairs, tinvs):
        rhs = []
        for hv in (pr["f"], pr["b"]):
            hv["eg"] = jnp.exp(hv["gcol"])
            rhs.append(jnp.concatenate([hv["vc"] * hv["bcol"], hv["kc"] * hv["bcol"] * hv["eg"]],
                                       axis=1).astype(_BF16))
        rhs2 = jnp.concatenate([jnp.concatenate([rhs[0], zero_rhs], axis=1),
                                jnp.concatenate([zero_rhs, rhs[1]], axis=1)], axis=0)
        uws.append(_dot(tinv.astype(_BF16), rhs2))
    yield
    mcs = []
    for pr, uw in zip(pairs, uws):
        for hv, off in ((pr["f"], 0), (pr["b"], 2 * d)):
            kdt = (hv["kc"] * jnp.exp(hv["glast"] - hv["gcol"])).T.astype(_BF16)
            mcs.append(_dot(kdt, uw[:, off:off + 2 * d].astype(_BF16)))
    yield
    intra_ref = ring_set[-1]
    for i, (pr, uw) in enumerate(zip(pairs, uws)):
        slot = pr["slot"]
        for hv, off, mc, ring in ((pr["f"], 0, mcs[2 * i], ring_set[0:4]), (pr["b"], 2 * d, mcs[2 * i + 1], ring_set[4:8])):
            lhs_ref, c_ref, u_ref, dec_ref = ring
            lhs_ref[pl.ds(slot * 4 * c, 2 * c), :] = (-mc[:, d:]).astype(_BF16)
            lhs_ref[pl.ds(slot * 4 * c + 2 * c, c), :] = uw[:, off + d:off + 2 * d].astype(_BF16)
            lhs_ref[pl.ds(slot * 4 * c + 3 * c, c), :] = (hv["qc"] * hv["eg"]).astype(_BF16)
            c_ref[pl.ds(slot * 2 * c, 2 * c), :] = mc[:, :d].astype(_BF16)
            u_ref[pl.ds(slot * c, c), :] = uw[:, off:off + d]
            dec_ref[pl.ds(slot, 1), :] = jnp.broadcast_to(jnp.exp(hv["glast"]), (1, d))
        intra_ref[pl.ds(slot * c, c), :] = jnp.where(incl, pr["qk"] * pr["decay"], 0.0).astype(_BF16)


def _gdn_scan_group(states, cidx_lists, ring_set, out_refs):
    c = GDN_CHUNK
    rings = (ring_set[0:4], ring_set[4:8])
    intra_ref = ring_set[-1]
    zero_v = jnp.zeros((c, HEAD_DIM), _BF16)
    for slot in range(GDN_GROUP):
        prods = [_dot(ring[0][pl.ds(slot * 4 * c, 4 * c), :], s.astype(_BF16)) for ring, s in zip(rings, states)]
        states = [s * ring[3][pl.ds(slot, 1), :] + p[:2 * c] + ring[1][pl.ds(slot * 2 * c, 2 * c), :].astype(_F32)
                  for ring, s, p in zip(rings, states, prods)]
        intra = intra_ref[pl.ds(slot * c, c), :]
        for forward, ring, cidxs, out_ref, p in zip((True, False), rings, cidx_lists, out_refs, prods):
            v16 = (ring[2][pl.ds(slot * c, c), :] - p[2 * c:3 * c]).astype(_BF16)
            v_pad = jnp.concatenate([v16, zero_v] if forward else [zero_v, v16], axis=0)
            r0 = pl.multiple_of(cidxs[slot] * c, c)
            out_ref[pl.ds(r0, c), :] = p[3 * c:] + _dot(intra, v_pad)
        yield
    return tuple(states)


def _gdn_kernel(q_ref, k_ref, v_ref, cwq_ref, cwk_ref, cwv_ref, slab_ref, grow_ref, z_ref, nw_ref,
                o_ref, qn, kn, vn, of, ob, *ring_refs):
    head = pl.program_id(1)
    seq = q_ref.shape[1]
    n_chunks = seq // GDN_CHUNK
    n_groups = n_chunks // GDN_GROUP
    n_conv = n_groups // 2
    assert n_groups % 4 == 0 and (GDN_GROUP * GDN_CHUNK) % GDN_CONV_ROWS == 0

    def conv(g):
        span = GDN_GROUP * GDN_CHUNK
        starts = [base + t for base in (g * span, seq - (g + 1) * span) for t in range(0, span, GDN_CONV_ROWS)]
        return _conv_silu_tiles(starts, (q_ref, k_ref, v_ref), (cwq_ref, cwk_ref, cwv_ref), (qn, kn, vn),
                                (HEAD_DIM ** -0.5, 1.0, None))

    nf = GDN_SET_FIELDS
    sets = [tuple(ring_refs[s * nf:(s + 1) * nf]) for s in range(2)]

    def groups(n):
        return ([n * GDN_GROUP + j for j in range(GDN_GROUP)],
                [n_chunks - 1 - n * GDN_GROUP - j for j in range(GDN_GROUP)])

    def prepare(n, ring_set):
        return _gdn_prepare_group(groups(n), head=head, qn=qn, kn=kn, vn=vn, slab_ref=slab_ref,
                                  grow_ref=grow_ref, ring_set=ring_set)

    def scan(n, states, ring_set):
        return _gdn_scan_group(states, groups(n), ring_set, (of, ob))

    def stage(n, states, *side_work):
        return _interleave(scan(n, states, sets[n % 2]), prepare(n + 1, sets[(n + 1) % 2]), *side_work)[0]

    def stage_pair(m, states):
        n = 2 * m
        states = _interleave(scan(n, states, sets[0]), prepare(n + 1, sets[1]))[0]
        return _interleave(scan(n + 1, states, sets[1]), prepare(n + 2, sets[0]))[0]

    zero = jnp.zeros((HEAD_DIM, HEAD_DIM), _F32)
    _interleave(conv(0))
    _interleave(prepare(0, sets[0]), conv(1))
    states = (zero, zero)
    for n in range(n_conv - 2):
        states = stage(n, states, conv(n + 2))
    states = lax.fori_loop((n_conv - 2) // 2, n_groups // 2 - 1, stage_pair, states)

    rows = 256
    span = GDN_GROUP * GDN_CHUNK

    def finish(lo, hi):
        for start in range(lo, hi, rows):
            sl = pl.ds(start, rows)
            o = of[sl, :] + ob[sl, :]
            o = o * lax.rsqrt(jnp.mean(o * o, axis=-1, keepdims=True) + NORM_EPS) * nw_ref[...]
            z = z_ref[0, sl, :].astype(_F32)
            o_ref[0, sl, :] = (o * (z * _sigmoid(z))).astype(o_ref.dtype)
            yield

    g = n_groups - 2
    states = stage(g, states, finish(seq - g * span, g * span))
    g = n_groups - 1
    _interleave(scan(g, states, sets[g % 2]), finish(seq - g * span, seq - (g - 1) * span),
                finish((g - 1) * span, g * span))
    _interleave(finish(0, seq - g * span), finish(g * span, seq))


def _gdn_mixer(qkv, z, slab, conv_w, norm_w):
    bsz, seq, _ = qkv.shape
    c = GDN_CHUNK
    nc = seq // c
    h = GDN_HEADS
    slab_t = slab.transpose(0, 2, 1).reshape(bsz, LANES, nc, c)
    grow = jnp.concatenate([slab_t[:, :h], slab_t[:, h:2 * h, ::-1]], axis=-1)
    col = lambda off: pl.BlockSpec((1, seq, HEAD_DIM), functools.partial(lambda b, hh, o: (b, 0, o + hh), o=off))
    cw = lambda off: pl.BlockSpec((3, HEAD_DIM), functools.partial(lambda b, hh, o: (0, o + hh), o=off))
    return pl.pallas_call(
        _gdn_kernel,
        grid=(bsz, h),
        in_specs=[col(0), col(h), col(2 * h), cw(0), cw(h), cw(2 * h),
                  pl.BlockSpec((1, seq, LANES), lambda b, hh: (b, 0, 0)),
                  pl.BlockSpec((1, 1, nc, 2 * c), lambda b, hh: (b, hh, 0, 0)),
                  col(0),
                  pl.BlockSpec((1, HEAD_DIM), lambda b, hh: (0, 0))],
        out_specs=col(0),
        out_shape=jax.ShapeDtypeStruct((bsz, seq, GDN_WIDTH), _BF16),
        scratch_shapes=([pltpu.VMEM((seq, HEAD_DIM), _F32)] * 5
                        + _gdn_set_shapes() * 2),
        compiler_params=_params("arbitrary", "arbitrary"),
        name="gdn_mixer",
    )(qkv, qkv, qkv, conv_w, conv_w, conv_w, slab, grow, z, norm_w.reshape(1, HEAD_DIM).astype(_F32))


ATT_RADIUS = 64
ATT_BQ = 128
ATT_WIDTH_KEYS = ATT_BQ + 2 * ATT_RADIUS
ATT_GROUP = 8
ATT_SPLIT = 4
assert all(w // (2 * d) == ATT_RADIUS for w, d in DILATED_PATTERNS)
assert tuple(d for _, d in DILATED_PATTERNS) == (1, ATT_SPLIT, ATT_SPLIT * ATT_SPLIT)


def _attn_blocks(qs, kws, vws, biases):
    scale = HEAD_DIM ** -0.5
    ss = [_dot_nt(q, kw) * scale + b for q, kw, b in zip(qs, kws, biases)]
    ms = [jnp.max(s, axis=-1, keepdims=True) for s in ss]
    es = [jnp.exp(s - m) for s, m in zip(ss, ms)]
    ls = [jnp.sum(e, axis=-1, keepdims=True) for e in es]
    accs = [_dot(e.astype(_BF16), vw) for e, vw in zip(es, vws)]
    return accs, ms, ls


def _attn_merge(acc_a, m_a, l_a, acc_b, m_b, l_b):
    m = jnp.maximum(m_a, m_b)
    wa = jnp.exp(m_a - m)
    wb = jnp.exp(m_b - m)
    return wa * acc_a + wb * acc_b, m, wa * l_a + wb * l_b


def _attn_kernel(slope_ref, q_ref, k_ref, v_ref, o_ref, nat, xq, xk, xv, acc4, m4, l4, acc1, m1, l1, bias_scr):
    seq = q_ref.shape[1]
    bq, width, radius, split, group = ATT_BQ, ATT_WIDTH_KEYS, ATT_RADIUS, ATT_SPLIT, ATT_GROUP
    n4 = seq // split
    n16 = n4 // split
    rep = lambda x: jnp.broadcast_to(x, (bq, HEAD_DIM))

    slope = slope_ref[0][:, 0:1]
    rel = (lax.broadcasted_iota(jnp.int32, (bq, width), 1) - lax.broadcasted_iota(jnp.int32, (bq, width), 0))
    for p, (_, dilation) in enumerate(DILATED_PATTERNS):
        for pos in range(3):
            dist = jnp.abs(rel - pos * radius)
            bias_scr[p, pos] = jnp.where(dist <= radius, -(slope * float(dilation)) * dist.astype(_F32), NEG_INF)

    def window(i0, n):
        if isinstance(i0, int):
            ws = min(max(i0 - radius, 0), n - width)
        else:
            ws = pl.multiple_of(jnp.clip(i0 - radius, 0, n - width), radius)
        return ws, (i0 - ws) // radius

    rows = 512

    def split_classes(tiles_per_stage):
        done = 0
        for src, dst in ((q_ref, xq), (k_ref, xk), (v_ref, xv)):
            for r0 in range(0, seq, rows):
                nat[pl.ds(r0, rows), :] = src[0, pl.ds(r0, rows), :].astype(_F32)
                done += 1
                if done % tiles_per_stage == 0:
                    yield
            for cls in range(split):
                for r0 in range(0, n4, rows):
                    dst[cls, pl.ds(r0, rows), :] = nat[pl.ds(cls + split * r0, rows, stride=split), :]
                    done += 1
                    if done % tiles_per_stage == 0:
                        yield

    def dil1_steps():
        for i in range(seq // (bq * group)):
            qs, kws, vws, biases, i0s = [], [], [], [], []
            for j in range(group):
                i0 = (i * group + j) * bq
                ws, pos = window(i0, seq)
                i0s.append(i0)
                qs.append(q_ref[0, pl.ds(i0, bq), :])
                kws.append(k_ref[0, pl.ds(ws, width), :])
                vws.append(v_ref[0, pl.ds(ws, width), :])
                biases.append(bias_scr[0, pos])
            accs, ms, ls = _attn_blocks(qs, kws, vws, biases)
            for j in range(group):
                acc1[pl.ds(i0s[j], bq), :] = accs[j]
                m1[pl.ds(i0s[j], bq), :] = rep(ms[j])
                l1[pl.ds(i0s[j], bq), :] = rep(ls[j])
            yield

    n_tiles = 3 * (seq // rows + split * (n4 // rows))
    _interleave(dil1_steps(), split_classes(n_tiles // (seq // (bq * group))))

    per_step = group // split

    def dil4(i, carry):
        qs, kws, vws, biases, q_rows = [], [], [], [], []
        for j in range(per_step):
            i0 = pl.multiple_of((i * per_step + j) * bq, bq)
            ws, pos = window(i0, n4)
            for c in range(split):
                q_rows.append((c, pl.ds(i0, bq)))
                qs.append(xq[c, pl.ds(i0, bq), :].astype(_BF16))
                kws.append(xk[c, pl.ds(ws, width), :].astype(_BF16))
                vws.append(xv[c, pl.ds(ws, width), :].astype(_BF16))
                biases.append(bias_scr[1, pos])
        accs, ms, ls = _attn_blocks(qs, kws, vws, biases)
        for (c, rows_), acc, m, l in zip(q_rows, accs, ms, ls):
            acc4[c, rows_, :] = acc
            m4[c, rows_, :] = rep(m)
            l4[c, rows_, :] = rep(l)
        return carry

    lax.fori_loop(0, n4 // (bq * per_step), dil4, 0)

    def dil16(i, carry):
        qs, kws, vws, biases, q_rows = [], [], [], [], []
        for j in range(per_step):
            blk = i * per_step + j
            sub = blk // (n16 // bq)
            i0 = pl.multiple_of((blk % (n16 // bq)) * bq, bq)
            ws, pos = window(i0, n16)
            k_rows = pl.ds(sub + split * ws, width, stride=split)
            for c in range(split):
                q_rows.append((c, pl.ds(sub + split * i0, bq, stride=split)))
                qs.append(xq[c, q_rows[-1][1], :].astype(_BF16))
                kws.append(xk[c, k_rows, :].astype(_BF16))
                vws.append(xv[c, k_rows, :].astype(_BF16))
                biases.append(bias_scr[2, pos])
        accs, ms, ls = _attn_blocks(qs, kws, vws, biases)
        for (c, rows_), acc_b, m_b, l_b in zip(q_rows, accs, ms, ls):
            acc, m, l = _attn_merge(acc4[c, rows_, :], m4[c, rows_, :], l4[c, rows_, :], acc_b, m_b, l_b)
            acc4[c, rows_, :] = acc
            m4[c, rows_, :] = m
            l4[c, rows_, :] = l
        return carry

    lax.fori_loop(0, split * (n16 // bq) // per_step, dil16, 0)

    for cls in range(split):
        def finish(r, carry, cls=cls):
            r0 = pl.multiple_of(r * bq, bq)
            nat_rows = pl.ds(cls + split * r0, bq, stride=split)
            acc, _, l = _attn_merge(acc1[nat_rows, :], m1[nat_rows, :], l1[nat_rows, :],
                                    acc4[cls, pl.ds(r0, bq), :], m4[cls, pl.ds(r0, bq), :], l4[cls, pl.ds(r0, bq), :])
            nat[nat_rows, :] = acc / l
            return carry
        lax.fori_loop(0, n4 // bq, finish, 0)

    def narrow(r, carry):
        sl = pl.ds(pl.multiple_of(r * rows, rows), rows)
        o_ref[0, sl, :] = nat[sl, :].astype(o_ref.dtype)
        return carry

    lax.fori_loop(0, seq // rows, narrow, 0)


def _dilated_attention(qkv, slopes):
    bsz, seq, _ = qkv.shape
    h = ATT_HEADS
    n4 = seq // ATT_SPLIT
    n16 = n4 // ATT_SPLIT
    assert n16 % ATT_BQ == 0 and n16 >= ATT_WIDTH_KEYS and seq % (ATT_BQ * ATT_GROUP) == 0
    assert ATT_GROUP % ATT_SPLIT == 0 and (ATT_SPLIT * n16 // ATT_BQ) % (ATT_GROUP // ATT_SPLIT) == 0
    col = lambda off: pl.BlockSpec((1, seq, HEAD_DIM), functools.partial(lambda b, hh, o: (b, 0, o + hh), o=off))
    slab = pltpu.VMEM((ATT_SPLIT, n4, HEAD_DIM), _F32)
    nat = pltpu.VMEM((seq, HEAD_DIM), _F32)
    return pl.pallas_call(
        _attn_kernel,
        grid=(bsz, h),
        in_specs=[pl.BlockSpec((1, 1, LANES), lambda b, hh: (hh, 0, 0)), col(0), col(h), col(2 * h)],
        out_specs=col(0),
        out_shape=jax.ShapeDtypeStruct((bsz, seq, ATT_WIDTH), _BF16),
        scratch_shapes=[nat, slab, slab, slab, slab, slab, slab, nat, nat, nat,
                        pltpu.VMEM((len(DILATED_PATTERNS), 3, ATT_BQ, ATT_WIDTH_KEYS), _F32)],
        compiler_params=_params("arbitrary", "arbitrary"),
        name="dilated_attention",
    )(slopes, qkv, qkv, qkv)


FFN_EPILOGUE_ROWS = 32

def _ffn_up_kernel(a_ref, wg_ref, wu_ref, cg_ref, cu_ref, wd_ref, o_ref, wd16_ref, wcat, y_even, y_odd, *,
                   n_tiles, tiles_per_seq, halo):
    s = pl.program_id(0)
    n_steps = pl.num_programs(0) - 1
    tn = wg_ref.shape[1]
    tm = o_ref.shape[0]
    wd16_ref[...] = wd_ref[...].astype(wd16_ref.dtype)

    @pl.when(jnp.logical_and(s % n_tiles == 0, s < n_steps))
    def _():
        _cast_rows(wg_ref, wcat.at[:, pl.ds(0, tn)])
        _cast_rows(wu_ref, wcat.at[:, pl.ds(tn, tn)])

    @pl.when(s == 0)
    def _():
        y_odd[...] = jnp.zeros(y_odd.shape, y_odd.dtype)

    tile = jnp.maximum(s - 1, 0) % n_tiles
    first = (tile % tiles_per_seq) == 0
    last = (tile % tiles_per_seq) == tiles_per_seq - 1

    def step(y_new, y_old):
        cw = jnp.concatenate([cg_ref[...], cu_ref[...]], axis=1)
        rows = FFN_EPILOGUE_ROWS
        n_chunks = tm // rows

        def finish(c):
            r0 = halo + c * rows
            y = y_old[pl.ds(r0, rows), :]
            prev_row = y_old[pl.ds(r0 - 8, 8), :][7:8, :]
            next_row = y_old[pl.ds(r0 + rows, 8), :][0:1, :]
            if c == 0:
                prev_row = jnp.where(first, 0.0, prev_row)
            if c == n_chunks - 1:
                next_row = jnp.where(last, 0.0, next_row)
            ym1, yp1 = _shift_rows(y, prev_row, next_row)
            conv = ym1 * cw[0:1, :] + y * cw[1:2, :] + yp1 * cw[2:3, :]
            gate = conv[:, :tn]
            up = conv[:, tn:]
            o_ref[pl.ds(c * rows, rows), :] = (gate * _sigmoid(gate) * up).astype(o_ref.dtype)

        proj = _dot(a_ref[...], wcat[...])
        for c in range(n_chunks):
            finish(c)
        y_new[...] = proj

    @pl.when(s % 2 == 0)
    def _():
        step(y_even, y_odd)

    @pl.when(s % 2 == 1)
    def _():
        step(y_odd, y_even)


def _ffn_up(u_pad, w_up, conv_w, w_down, seq, pad_rows, tm=1024, tn=256):
    m, k = u_pad.shape[0] - 2 * pad_rows, u_pad.shape[1]
    ffn = w_up.shape[1] // 2
    halo = BF16_ROWS
    assert ffn % tn == 0 and seq % tm == 0 and pad_rows >= halo
    nj = ffn // tn
    n_tiles = m // tm
    n_steps = nj * n_tiles
    wd_rows = w_down.shape[0] // n_steps
    assert wd_rows * n_steps == w_down.shape[0] and wd_rows % BF16_ROWS == 0
    wd_spec = pl.BlockSpec((wd_rows, w_down.shape[1]), lambda s: (jnp.minimum(s, n_steps - 1), 0))
    assert tm % halo == 0 and pad_rows % halo == 0
    col = lambda s: jnp.minimum(s // n_tiles, nj - 1)
    col_prev = lambda s: jnp.maximum(s - 1, 0) // n_tiles
    a_spec = pl.BlockSpec((pl.Element(tm + 2 * halo), pl.Element(k)),
                          lambda s: (((s % n_tiles) * (tm // halo) + (pad_rows // halo - 1)) * halo, 0))
    return pl.pallas_call(
        functools.partial(_ffn_up_kernel, n_tiles=n_tiles, tiles_per_seq=seq // tm, halo=halo),
        grid=(n_steps + 1,),
        in_specs=[a_spec,
                  pl.BlockSpec((k, tn), lambda s: (0, col(s))),
                  pl.BlockSpec((k, tn), lambda s: (0, col(s) + nj)),
                  pl.BlockSpec((3, tn), lambda s: (0, col_prev(s))),
                  pl.BlockSpec((3, tn), lambda s: (0, col_prev(s) + nj)),
                  wd_spec],
        out_specs=[pl.BlockSpec((tm, tn), lambda s: (jnp.maximum(s - 1, 0) % n_tiles, col_prev(s))), wd_spec],
        out_shape=[jax.ShapeDtypeStruct((m, ffn), _BF16), jax.ShapeDtypeStruct(w_down.shape, _BF16)],
        scratch_shapes=[pltpu.VMEM((k, 2 * tn), _BF16)] + [pltpu.VMEM((tm + 2 * halo, 2 * tn), _F32)] * 2,
        compiler_params=_params("arbitrary"),
        name="ffn_up_conv_glu",
    )(u_pad, w_up, w_up, conv_w, conv_w, w_down)


def _ple_kernel(a_ref, wg_ref, p_ref, wp_ref, h_ref, o_ref, wgb, wpb):
    @pl.when(pl.program_id(1) == 0)
    def _():
        _cast_rows(wg_ref, wgb)
        _cast_rows(wp_ref, wpb)

    gate = _sigmoid(_dot(a_ref[...], wgb[...]))
    emb = _dot(p_ref[...], wpb[...])
    o_ref[...] = h_ref[...] + gate * emb


def _ple(u, w_gate, p, w_proj, h, *, tm, tn):
    m, k = u.shape
    n = w_gate.shape[1]
    kp = p.shape[1]
    return pl.pallas_call(
        _ple_kernel,
        grid=(n // tn, m // tm),
        in_specs=[pl.BlockSpec((tm, k), lambda j, i: (i, 0)),
                  pl.BlockSpec((k, tn), lambda j, i: (0, j)),
                  pl.BlockSpec((tm, kp), lambda j, i: (i, 0)),
                  pl.BlockSpec((kp, tn), lambda j, i: (0, j)),
                  pl.BlockSpec((tm, tn), lambda j, i: (i, j))],
        out_specs=pl.BlockSpec((tm, tn), lambda j, i: (i, j)),
        out_shape=jax.ShapeDtypeStruct((m, n), _F32),
        scratch_shapes=[pltpu.VMEM((k, tn), _BF16), pltpu.VMEM((kp, tn), _BF16)],
        compiler_params=_params("arbitrary", "arbitrary"),
        name="ple_gate",
    )(u, w_gate, p, w_proj, h)


def kernel(x, p, attn_norm, w_in, gdn_conv, gdn_a_log, gdn_dt_bias, gdn_out_norm, w_out, ffn_norm, w_up,
           ffn_conv, w_down, ple_norm, w_ple_gate, w_ple_proj, final_norm):
    bsz, seq, d_model = x.shape
    depth = w_in.shape[0]
    m = bsz * seq
    qkv_cols = 3 * GDN_WIDTH
    ab_off = qkv_cols + GDN_WIDTH
    att_off = ab_off + 4 * GDN_HEADS
    slopes = jnp.exp2(-ALIBI_MAX_BIAS * (jnp.arange(ATT_HEADS, dtype=_F32) + 1.0) / ATT_HEADS)
    slopes = jnp.broadcast_to(slopes[:, None, None], (ATT_HEADS, 1, LANES))

    h = x.reshape(m, d_model)
    for i in range(depth):
        u = _rmsnorm(h, attn_norm[i], _BF16)
        wt = jnp.swapaxes(w_in[i], 0, 1)
        tm, tn = MM_TM, MM_TN
        gdn_qkv = _matmul_nt(u, wt, n_cols=qkv_cols, row_off=0, tm=tm, tn=tn, out_dtype=_F32, name="proj_gdn_qkv")
        gdn_z = _matmul_nt(u, wt, n_cols=GDN_WIDTH, row_off=qkv_cols, tm=tm, tn=tn, out_dtype=_BF16,
                           name="proj_gdn_z")
        ab = _matmul_nt(u, wt, n_cols=LANES, row_off=ab_off, tm=tm, tn=LANES, out_dtype=_F32, name="proj_gdn_ab")
        att_qkv = _matmul_nt(u, wt, n_cols=3 * ATT_WIDTH, row_off=att_off, tm=tm, tn=tn, out_dtype=_BF16,
                             name="proj_att_qkv")

        slab = _gdn_gates(ab, gdn_a_log[i], gdn_dt_bias[i]).reshape(bsz, seq, LANES)
        o_gdn = _gdn_mixer(gdn_qkv.reshape(bsz, seq, qkv_cols), gdn_z.reshape(bsz, seq, GDN_WIDTH), slab,
                           gdn_conv[i], gdn_out_norm[i]).reshape(m, GDN_WIDTH)

        att3 = att_qkv.reshape(bsz, seq, 3 * ATT_WIDTH)
        o_att = _dilated_attention(att3, slopes).reshape(m, ATT_WIDTH)

        h = _matmul([o_gdn, o_att], w_out[i], n_cols=d_model, tm=tm, tn=tn, out_dtype=_F32, residual=h,
                    name="out_proj")

        u_pad = _rmsnorm(h, ffn_norm[i], _BF16, pad_tiles=1)
        act, w_down16 = _ffn_up(u_pad, w_up[i], ffn_conv[i], w_down[i], seq, pad_rows=NORM_TM)
        h = _matmul([act], w_down16, n_cols=d_model, tm=MM_TM_LONG_K, tn=tn, out_dtype=_F32,
                    residual=h, name="ffn_down")

        u = _rmsnorm(h, ple_norm[i], _BF16)
        h = _ple(u, w_ple_gate[i], p[i].reshape(m, -1).astype(_BF16), w_ple_proj[i], h, tm=tm, tn=tn)
    return _rmsnorm(h, final_norm, x.dtype).reshape(bsz, seq, d_model)
```

```python
import functools

import jax
import jax.numpy as jnp
from jax import lax
from jax.experimental import pallas as pl
from jax.experimental.pallas import tpu as pltpu

HEAD_DIM = 128
GDN_HEADS = 16
ATT_HEADS = 16
GDN_WIDTH = GDN_HEADS * HEAD_DIM
ATT_WIDTH = ATT_HEADS * HEAD_DIM
GDN_CHUNK = 64
DILATED_PATTERNS = ((128, 1), (512, 4), (2048, 16))
ALIBI_MAX_BIAS = 8.0
NORM_EPS = 1e-6
NEG_INF = -1e30

LANES = 128
SUBLANES = 8
BF16_ROWS = 16
VMEM_LIMIT_BYTES = 56 * 1024 * 1024
MM_TM = 1024
MM_TN = 512
MM_TM_LONG_K = 512
NORM_TM = 512

_F32 = jnp.float32
_BF16 = jnp.bfloat16


def _params(*semantics):
    return pltpu.CompilerParams(dimension_semantics=semantics, vmem_limit_bytes=VMEM_LIMIT_BYTES)


def _sigmoid(x):
    return 1.0 / (1.0 + jnp.exp(-x))


def _dot(a, b):
    return jnp.dot(a, b, preferred_element_type=_F32)


def _dot_nt(a, b):
    return lax.dot_general(a, b, (((1,), (1,)), ((), ())), preferred_element_type=_F32)


def _rmsnorm_kernel(x_ref, w_ref, o_ref, *, pad_tiles):
    x = x_ref[...].astype(_F32)
    ms = jnp.mean(x * x, axis=-1, keepdims=True)
    y = x * lax.rsqrt(ms + NORM_EPS) * w_ref[...]
    if pad_tiles:
        i = pl.program_id(0)
        inside = jnp.logical_and(i >= pad_tiles, i < pl.num_programs(0) - pad_tiles)
        y = jnp.where(inside, y, 0.0)
    o_ref[...] = y.astype(o_ref.dtype)


def _rmsnorm(x, w, out_dtype, tm=NORM_TM, pad_tiles=0):
    m, d = x.shape
    nt = m // tm
    return pl.pallas_call(
        functools.partial(_rmsnorm_kernel, pad_tiles=pad_tiles),
        grid=(nt + 2 * pad_tiles,),
        in_specs=[pl.BlockSpec((tm, d), lambda i: (jnp.clip(i - pad_tiles, 0, nt - 1), 0)),
                  pl.BlockSpec((1, d), lambda i: (0, 0))],
        out_specs=pl.BlockSpec((tm, d), lambda i: (i, 0)),
        out_shape=jax.ShapeDtypeStruct((m + 2 * pad_tiles * tm, d), out_dtype),
        compiler_params=_params("arbitrary"),
        name="rmsnorm",
    )(x, w.reshape(1, d).astype(_F32))


def _cast_rows(src_ref, dst_ref, rows=256):
    def body(r, carry):
        sl = pl.ds(pl.multiple_of(r * rows, rows), rows)
        dst_ref[sl, :] = src_ref[sl, :].astype(dst_ref.dtype)
        return carry
    lax.fori_loop(0, src_ref.shape[0] // rows, body, 0)


def _mm_kernel(*refs, n_a, has_res, cast_w):
    a_refs = refs[:n_a]
    w_refs = refs[n_a:2 * n_a]
    pos = 2 * n_a
    res_ref = refs[pos] if has_res else None
    pos += int(has_res)
    o_ref = refs[pos]
    wb_refs = refs[pos + 1:pos + 1 + n_a] if cast_w else w_refs
    if cast_w:
        @pl.when(pl.program_id(1) == 0)
        def _():
            for w_ref, wb_ref in zip(w_refs, wb_refs):
                _cast_rows(w_ref, wb_ref)
    acc = None
    for a_ref, wb_ref in zip(a_refs, wb_refs):
        part = _dot(a_ref[...], wb_ref[...])
        acc = part if acc is None else acc + part
    if has_res:
        acc = acc + res_ref[...]
    o_ref[...] = acc.astype(o_ref.dtype)


def _mm_nt_kernel(a_ref, wt_ref, o_ref, wb_ref):
    @pl.when(pl.program_id(1) == 0)
    def _():
        _cast_rows(wt_ref, wb_ref, rows=SUBLANES * 8)

    o_ref[...] = _dot_nt(a_ref[...], wb_ref[...]).astype(o_ref.dtype)


def _matmul_nt(a, wt, *, n_cols, row_off, tm, tn, out_dtype, name):
    m, k = a.shape
    assert n_cols % tn == 0 and m % tm == 0 and row_off % SUBLANES == 0 and tn % SUBLANES == 0
    return pl.pallas_call(
        _mm_nt_kernel,
        grid=(n_cols // tn, m // tm),
        in_specs=[pl.BlockSpec((tm, k), lambda j, i: (i, 0)),
                  pl.BlockSpec((pl.Element(tn), pl.Element(k)),
                               lambda j, i: ((j * (tn // SUBLANES) + row_off // SUBLANES) * SUBLANES, 0))],
        out_specs=pl.BlockSpec((tm, tn), lambda j, i: (i, j)),
        out_shape=jax.ShapeDtypeStruct((m, n_cols), out_dtype),
        scratch_shapes=[pltpu.VMEM((tn, k), _BF16)],
        compiler_params=_params("arbitrary", "arbitrary"),
        name=name,
    )(a, wt)


def _matmul(a_list, w, *, n_cols, col_off=0, tm, tn, out_dtype, residual=None, name):
    m = a_list[0].shape[0]
    assert col_off % tn == 0 and n_cols % tn == 0 and m % tm == 0
    off = col_off // tn
    cast_w = w.dtype != _BF16
    in_specs, scratch = [], []
    for a in a_list:
        in_specs.append(pl.BlockSpec((tm, a.shape[1]), lambda j, i: (i, 0)))
    row_blk = 0
    for a in a_list:
        kk = a.shape[1]
        assert all(b.shape[1] == kk for b in a_list)
        in_specs.append(pl.BlockSpec((kk, tn), functools.partial(lambda j, i, rb: (rb, j + off), rb=row_blk)))
        if cast_w:
            scratch.append(pltpu.VMEM((kk, tn), _BF16))
        row_blk += 1
    args = list(a_list) + [w] * len(a_list)
    if residual is not None:
        in_specs.append(pl.BlockSpec((tm, tn), lambda j, i: (i, j)))
        args.append(residual)
    return pl.pallas_call(
        functools.partial(_mm_kernel, n_a=len(a_list), has_res=residual is not None, cast_w=cast_w),
        grid=(n_cols // tn, m // tm),
        in_specs=in_specs,
        out_specs=pl.BlockSpec((tm, tn), lambda j, i: (i, j)),
        out_shape=jax.ShapeDtypeStruct((m, n_cols), out_dtype),
        scratch_shapes=scratch,
        compiler_params=_params("arbitrary", "arbitrary"),
        name=name,
    )(*args)


def _gate_kernel(ab_ref, alog_ref, dtb_ref, o_ref):
    x = ab_ref[...]
    rows = x.shape[0]
    xa = x + dtb_ref[...]
    softplus = jnp.maximum(xa, 0.0) + jnp.log(1.0 + jnp.exp(-jnp.abs(xa)))
    g = -jnp.exp(alog_ref[...]) * softplus
    beta = _sigmoid(x)
    c = GDN_CHUNK
    ri = lax.broadcasted_iota(jnp.int32, (c, c), 0)
    ci = lax.broadcasted_iota(jnp.int32, (c, c), 1)
    lower = (ri >= ci).astype(_F32)
    upper = (ri <= ci).astype(_F32)
    lane = lax.broadcasted_iota(jnp.int32, (c, LANES), 1)
    for n in range(rows // c):
        gc = g[n * c:(n + 1) * c]
        prefix = jnp.dot(lower, gc, precision=lax.Precision.HIGHEST, preferred_element_type=_F32)
        suffix = jnp.dot(upper, gc, precision=lax.Precision.HIGHEST, preferred_element_type=_F32)
        out = jnp.where(lane < GDN_HEADS, prefix,
                        jnp.where(lane < 2 * GDN_HEADS, suffix,
                                  jnp.where(lane < 4 * GDN_HEADS, beta[n * c:(n + 1) * c], 0.0)))
        o_ref[n * c:(n + 1) * c, :] = out


def _gdn_gates(ab, a_log, dt_bias, tm=512):
    m = ab.shape[0]
    pad = LANES - 2 * GDN_HEADS
    alog_vec = jnp.pad(a_log.reshape(-1).astype(_F32), (0, pad)).reshape(1, LANES)
    dtb_vec = jnp.pad(dt_bias.reshape(-1).astype(_F32), (0, pad)).reshape(1, LANES)
    vec_spec = pl.BlockSpec((1, LANES), lambda i: (0, 0))
    return pl.pallas_call(
        _gate_kernel,
        grid=(m // tm,),
        in_specs=[pl.BlockSpec((tm, LANES), lambda i: (i, 0)), vec_spec, vec_spec],
        out_specs=pl.BlockSpec((tm, LANES), lambda i: (i, 0)),
        out_shape=jax.ShapeDtypeStruct((m, LANES), _F32),
        compiler_params=_params("arbitrary"),
        name="gdn_gates",
    )(ab, alog_vec, dtb_vec)


GDN_CONV_ROWS = 256


def _shift_rows(x, prev_row, next_row):
    rows = x.shape[0]
    row = lax.broadcasted_iota(jnp.int32, x.shape, 0)
    xm1 = jnp.where(row == 0, prev_row, pltpu.roll(x, 1, 0))
    xp1 = jnp.where(row == rows - 1, next_row, pltpu.roll(x, rows - 1, 0))
    return xm1, xp1


def _conv_silu_tiles(starts, x_refs, w_refs, dst_refs, l2_scales, rows=GDN_CONV_ROWS):
    seq = x_refs[0].shape[1]
    ws = [w_ref[...] for w_ref in w_refs]
    for start in starts:
        t0 = pl.multiple_of(start, rows)
        p0 = pl.multiple_of(jnp.maximum(t0 - 8, 0), 8)
        n0 = pl.multiple_of(jnp.minimum(t0 + rows, seq - 8), 8)
        for x_ref, w, dst_ref, l2_scale in zip(x_refs, ws, dst_refs, l2_scales):
            x = x_ref[0, pl.ds(t0, rows), :]
            prev_row = jnp.where(t0 == 0, 0.0, x_ref[0, pl.ds(p0, 8), :][7:8, :])
            next_row = jnp.where(t0 == seq - rows, 0.0, x_ref[0, pl.ds(n0, 8), :][0:1, :])
            xm1, xp1 = _shift_rows(x, prev_row, next_row)
            y = xm1 * w[0:1, :] + x * w[1:2, :] + xp1 * w[2:3, :]
            y = y * _sigmoid(y)
            if l2_scale is not None:
                y = y * lax.rsqrt(jnp.sum(y * y, axis=-1, keepdims=True) + NORM_EPS) * l2_scale
            dst_ref[pl.ds(t0, rows), :] = y
        yield


def _unit_tri_inverses(mats):
    c = mats[0].shape[0]
    ri = lax.broadcasted_iota(jnp.int32, (c, 2 * c), 0)
    ci = lax.broadcasted_iota(jnp.int32, (c, 2 * c), 1)
    eye = jnp.where(jnp.logical_or(ri == ci, ri + c == ci), 1.0, 0.0)

    def blockdiag(x):
        return jnp.concatenate([jnp.where(ci < c, x, 0.0), jnp.where(ci >= c, x, 0.0)], axis=0).astype(_BF16)

    xs = [eye - a for a in mats]
    ps = [_dot(a.astype(_BF16), blockdiag(a)) for a in mats]
    yield
    power = 2
    while 2 * power < c:
        xps = [_dot(jnp.concatenate([x, p], axis=0).astype(_BF16), blockdiag(p)) for x, p in zip(xs, ps)]
        xs = [x + xp[:c] for x, xp in zip(xs, xps)]
        ps = [xp[c:] for xp in xps]
        power *= 2
        yield
    return [x + _dot(x.astype(_BF16), blockdiag(p)) for x, p in zip(xs, ps)]


def _interleave(*stages):
    results = [None] * len(stages)
    live = list(range(len(stages)))
    while live:
        for i in list(live):
            try:
                next(stages[i])
            except StopIteration as stop:
                results[i] = stop.value
                live.remove(i)
    return results


GDN_GROUP = 8
GDN_SET_FIELDS = 9


def _gdn_set_shapes():
    g, c, d = GDN_GROUP, GDN_CHUNK, HEAD_DIM
    per_dir = [pltpu.VMEM((g * 4 * c, d), _BF16), pltpu.VMEM((g * d, d), _BF16), pltpu.VMEM((g * c, d), _F32),
               pltpu.VMEM((g, d), _F32)]
    return per_dir * 2 + [pltpu.VMEM((g * c, 2 * c), _BF16)]


def _gdn_prepare_group(cidx_lists, *, head, qn, kn, vn, slab_ref, grow_ref, ring_set):
    c, d = GDN_CHUNK, HEAD_DIM
    ri = lax.broadcasted_iota(jnp.int32, (c, 2 * c), 0)
    ci = lax.broadcasted_iota(jnp.int32, (c, 2 * c), 1)
    left = ci < c
    cil = jnp.where(left, ci, ci - c)
    fwd_i = left.astype(jnp.int32)
    incl = (ri - cil) * (2 * fwd_i - 1) >= 0
    strict = (ri - cil) * (2 * fwd_i - 1) > 0
    lane = lax.broadcasted_iota(jnp.int32, (c, LANES), 1)
    zero_k = jnp.zeros((c, d), _BF16)
    zero_rhs = jnp.zeros((c, 2 * d), _BF16)
    pairs = []
    for slot, (cf, cb) in enumerate(zip(*cidx_lists)):
        halves = []
        for forward, cidx in ((True, cf), (False, cb)):
            r0 = pl.multiple_of(cidx * c, c)
            slab = slab_ref[0, pl.ds(r0, c), :]
            g_lane = head if forward else GDN_HEADS + head
            b_lane = 2 * GDN_HEADS + g_lane
            gcol = jnp.sum(jnp.where(lane == g_lane, slab, 0.0), axis=-1, keepdims=True)
            bcol = jnp.sum(jnp.where(lane == b_lane, slab, 0.0), axis=-1, keepdims=True)
            halves.append(dict(qc=qn[pl.ds(r0, c), :], kc=kn[pl.ds(r0, c), :], vc=vn[pl.ds(r0, c), :],
                               gcol=gcol, bcol=bcol, glast=gcol[c - 1:c, :] if forward else gcol[0:1, :]))
        f, b = halves
        k16f, k16b = f["kc"].astype(_BF16), b["kc"].astype(_BF16)
        lhs = jnp.concatenate([jnp.concatenate([f["qc"], b["qc"]], axis=1),
                               jnp.concatenate([f["kc"], b["kc"]], axis=1)], axis=0).astype(_BF16)
        rhs_t = jnp.concatenate([jnp.concatenate([k16f, zero_k], axis=1),
                                 jnp.concatenate([zero_k, k16b], axis=1)], axis=0)
        prod = _dot_nt(lhs, rhs_t)
        gcol2 = jnp.where(left, f["gcol"], b["gcol"])
        bcol2 = jnp.where(left, f["bcol"], b["bcol"])
        grow2 = grow_ref[0, 0, pl.ds(cf, 1), :]
        decay = jnp.where(incl, jnp.exp(jnp.where(incl, gcol2 - grow2, 0.0)), 0.0)
        pairs.append(dict(slot=slot, f=f, b=b, qk=prod[:c], decay=decay,
                          a=jnp.where(strict, prod[c:] * decay * bcol2, 0.0)))
    yield
    tinvs = yield from _unit_tri_inverses([pr["a"] for pr in pairs])
    yield
    uws = []
    for pr, tinv in zip(pairs, tinvs):
        rhs = []
        for hv in (pr["f"], pr["b"]):
            hv["eg"] = jnp.exp(hv["gcol"])
            rhs.append(jnp.concatenate([hv["vc"] * hv["bcol"], hv["kc"] * hv["bcol"] * hv["eg"]],
                                       axis=1).astype(_BF16))
        rhs2 = jnp.concatenate([jnp.concatenate([rhs[0], zero_rhs], axis=1),
                                jnp.concatenate([zero_rhs, rhs[1]], axis=1)], axis=0)
        uws.append(_dot(tinv.astype(_BF16), rhs2))
    yield
    mcs = []
    for pr, uw in zip(pairs, uws):
        for hv, off in ((pr["f"], 0), (pr["b"], 2 * d)):
            kdt = (hv["kc"] * jnp.exp(hv["glast"] - hv["gcol"])).T.astype(_BF16)
            mcs.append(_dot(kdt, uw[:, off:off + 2 * d].astype(_BF16)))
    yield
    intra_ref = ring_set[-1]
    for i, (pr, uw) in enumerate(zip(pairs, uws)):
        slot = pr["slot"]
        for hv, off, mc, ring in ((pr["f"], 0, mcs[2 * i], ring_set[0:4]), (pr["b"], 2 * d, mcs[2 * i + 1], ring_set[4:8])):
            lhs_ref, c_ref, u_ref, dec_ref = ring
            lhs_ref[pl.ds(slot * 4 * c, 2 * c), :] = (-mc[:, d:]).astype(_BF16)
            lhs_ref[pl.ds(slot * 4 * c + 2 * c, c), :] = uw[:, off + d:off + 2 * d].astype(_BF16)
            lhs_ref[pl.ds(slot * 4 * c + 3 * c, c), :] = (hv["qc"] * hv["eg"]).astype(_BF16)
            c_ref[pl.ds(slot * 2 * c, 2 * c), :] = mc[:, :d].astype(_BF16)
            u_ref[pl.ds(slot * c, c), :] = uw[:, off:off + d]
            dec_ref[pl.ds(slot, 1), :] = jnp.broadcast_to(jnp.exp(hv["glast"]), (1, d))
        intra_ref[pl.ds(slot * c, c), :] = jnp.where(incl, pr["qk"] * pr["decay"], 0.0).astype(_BF16)


def _gdn_scan_group(states, cidx_lists, ring_set, out_refs):
    c = GDN_CHUNK
    rings = (ring_set[0:4], ring_set[4:8])
    intra_ref = ring_set[-1]
    zero_v = jnp.zeros((c, HEAD_DIM), _BF16)
    for slot in range(GDN_GROUP):
        prods = [_dot(ring[0][pl.ds(slot * 4 * c, 4 * c), :], s.astype(_BF16)) for ring, s in zip(rings, states)]
        states = [s * ring[3][pl.ds(slot, 1), :] + p[:2 * c] + ring[1][pl.ds(slot * 2 * c, 2 * c), :].astype(_F32)
                  for ring, s, p in zip(rings, states, prods)]
        intra = intra_ref[pl.ds(slot * c, c), :]
        for forward, ring, cidxs, out_ref, p in zip((True, False), rings, cidx_lists, out_refs, prods):
            v16 = (ring[2][pl.ds(slot * c, c), :] - p[2 * c:3 * c]).astype(_BF16)
            v_pad = jnp.concatenate([v16, zero_v] if forward else [zero_v, v16], axis=0)
            r0 = pl.multiple_of(cidxs[slot] * c, c)
            out_ref[pl.ds(r0, c), :] = p[3 * c:] + _dot(intra, v_pad)
        yield
    return tuple(states)


def _gdn_kernel(q_ref, k_ref, v_ref, cwq_ref, cwk_ref, cwv_ref, slab_ref, grow_ref, z_ref, nw_ref,
                o_ref, qn, kn, vn, of, ob, *ring_refs):
    head = pl.program_id(1)
    seq = q_ref.shape[1]
    n_chunks = seq // GDN_CHUNK
    n_groups = n_chunks // GDN_GROUP
    n_conv = n_groups // 2
    assert n_groups % 4 == 0 and (GDN_GROUP * GDN_CHUNK) % GDN_CONV_ROWS == 0

    def conv(g):
        span = GDN_GROUP * GDN_CHUNK
        starts = [base + t for base in (g * span, seq - (g + 1) * span) for t in range(0, span, GDN_CONV_ROWS)]
        return _conv_silu_tiles(starts, (q_ref, k_ref, v_ref), (cwq_ref, cwk_ref, cwv_ref), (qn, kn, vn),
                                (HEAD_DIM ** -0.5, 1.0, None))

    nf = GDN_SET_FIELDS
    sets = [tuple(ring_refs[s * nf:(s + 1) * nf]) for s in range(2)]

    def groups(n):
        return ([n * GDN_GROUP + j for j in range(GDN_GROUP)],
                [n_chunks - 1 - n * GDN_GROUP - j for j in range(GDN_GROUP)])

    def prepare(n, ring_set):
        return _gdn_prepare_group(groups(n), head=head, qn=qn, kn=kn, vn=vn, slab_ref=slab_ref,
                                  grow_ref=grow_ref, ring_set=ring_set)

    def scan(n, states, ring_set):
        return _gdn_scan_group(states, groups(n), ring_set, (of, ob))

    def stage(n, states, *side_work):
        return _interleave(scan(n, states, sets[n % 2]), prepare(n + 1, sets[(n + 1) % 2]), *side_work)[0]

    def stage_pair(m, states):
        n = 2 * m
        states = _interleave(scan(n, states, sets[0]), prepare(n + 1, sets[1]))[0]
        return _interleave(scan(n + 1, states, sets[1]), prepare(n + 2, sets[0]))[0]

    zero = jnp.zeros((HEAD_DIM, HEAD_DIM), _F32)
    _interleave(conv(0))
    _interleave(prepare(0, sets[0]), conv(1))
    states = (zero, zero)
    for n in range(n_conv - 2):
        states = stage(n, states, conv(n + 2))
    states = lax.fori_loop((n_conv - 2) // 2, n_groups // 2 - 1, stage_pair, states)

    rows = 256
    span = GDN_GROUP * GDN_CHUNK

    def finish(lo, hi):
        for start in range(lo, hi, rows):
            sl = pl.ds(start, rows)
            o = of[sl, :] + ob[sl, :]
            o = o * lax.rsqrt(jnp.mean(o * o, axis=-1, keepdims=True) + NORM_EPS) * nw_ref[...]
            z = z_ref[0, sl, :].astype(_F32)
            o_ref[0, sl, :] = (o * (z * _sigmoid(z))).astype(o_ref.dtype)
            yield

    g = n_groups - 2
    states = stage(g, states, finish(seq - g * span, g * span))
    g = n_groups - 1
    _interleave(scan(g, states, sets[g % 2]), finish(seq - g * span, seq - (g - 1) * span),
                finish((g - 1) * span, g * span))
    _interleave(finish(0, seq - g * span), finish(g * span, seq))


def _gdn_mixer(qkv, z, slab, conv_w, norm_w):
    bsz, seq, _ = qkv.shape
    c = GDN_CHUNK
    nc = seq // c
    h = GDN_HEADS
    slab_t = slab.transpose(0, 2, 1).reshape(bsz, LANES, nc, c)
    grow = jnp.concatenate([slab_t[:, :h], slab_t[:, h:2 * h, ::-1]], axis=-1)
    col = lambda off: pl.BlockSpec((1, seq, HEAD_DIM), functools.partial(lambda b, hh, o: (b, 0, o + hh), o=off))
    cw = lambda off: pl.BlockSpec((3, HEAD_DIM), functools.partial(lambda b, hh, o: (0, o + hh), o=off))
    return pl.pallas_call(
        _gdn_kernel,
        grid=(bsz, h),
        in_specs=[col(0), col(h), col(2 * h), cw(0), cw(h), cw(2 * h),
                  pl.BlockSpec((1, seq, LANES), lambda b, hh: (b, 0, 0)),
                  pl.BlockSpec((1, 1, nc, 2 * c), lambda b, hh: (b, hh, 0, 0)),
                  col(0),
                  pl.BlockSpec((1, HEAD_DIM), lambda b, hh: (0, 0))],
        out_specs=col(0),
        out_shape=jax.ShapeDtypeStruct((bsz, seq, GDN_WIDTH), _BF16),
        scratch_shapes=([pltpu.VMEM((seq, HEAD_DIM), _F32)] * 5
                        + _gdn_set_shapes() * 2),
        compiler_params=_params("arbitrary", "arbitrary"),
        name="gdn_mixer",
    )(qkv, qkv, qkv, conv_w, conv_w, conv_w, slab, grow, z, norm_w.reshape(1, HEAD_DIM).astype(_F32))


ATT_RADIUS = 64
ATT_BQ = 128
ATT_WIDTH_KEYS = ATT_BQ + 2 * ATT_RADIUS
ATT_GROUP = 8
ATT_SPLIT = 4
assert all(w // (2 * d) == ATT_RADIUS for w, d in DILATED_PATTERNS)
assert tuple(d for _, d in DILATED_PATTERNS) == (1, ATT_SPLIT, ATT_SPLIT * ATT_SPLIT)


def _attn_blocks(qs, kws, vws, biases):
    scale = HEAD_DIM ** -0.5
    ss = [_dot_nt(q, kw) * scale + b for q, kw, b in zip(qs, kws, biases)]
    ms = [jnp.max(s, axis=-1, keepdims=True) for s in ss]
    es = [jnp.exp(s - m) for s, m in zip(ss, ms)]
    ls = [jnp.sum(e, axis=-1, keepdims=True) for e in es]
    accs = [_dot(e.astype(_BF16), vw) for e, vw in zip(es, vws)]
    return accs, ms, ls


def _attn_merge(acc_a, m_a, l_a, acc_b, m_b, l_b):
    m = jnp.maximum(m_a, m_b)
    wa = jnp.exp(m_a - m)
    wb = jnp.exp(m_b - m)
    return wa * acc_a + wb * acc_b, m, wa * l_a + wb * l_b


def _attn_kernel(slope_ref, q_ref, k_ref, v_ref, o_ref, nat, xq, xk, xv, acc4, m4, l4, acc1, m1, l1, bias_scr):
    seq = q_ref.shape[1]
    bq, width, radius, split, group = ATT_BQ, ATT_WIDTH_KEYS, ATT_RADIUS, ATT_SPLIT, ATT_GROUP
    n4 = seq // split
    n16 = n4 // split
    rep = lambda x: jnp.broadcast_to(x, (bq, HEAD_DIM))

    slope = slope_ref[0][:, 0:1]
    rel = (lax.broadcasted_iota(jnp.int32, (bq, width), 1) - lax.broadcasted_iota(jnp.int32, (bq, width), 0))
    for p, (_, dilation) in enumerate(DILATED_PATTERNS):
        for pos in range(3):
            dist = jnp.abs(rel - pos * radius)
            bias_scr[p, pos] = jnp.where(dist <= radius, -(slope * float(dilation)) * dist.astype(_F32), NEG_INF)

    def window(i0, n):
        if isinstance(i0, int):
            ws = min(max(i0 - radius, 0), n - width)
        else:
            ws = pl.multiple_of(jnp.clip(i0 - radius, 0, n - width), radius)
        return ws, (i0 - ws) // radius

    rows = 512

    def split_classes(tiles_per_stage):
        done = 0
        for src, dst in ((q_ref, xq), (k_ref, xk), (v_ref, xv)):
            for r0 in range(0, seq, rows):
                nat[pl.ds(r0, rows), :] = src[0, pl.ds(r0, rows), :].astype(_F32)
                done += 1
                if done % tiles_per_stage == 0:
                    yield
            for cls in range(split):
                for r0 in range(0, n4, rows):
                    dst[cls, pl.ds(r0, rows), :] = nat[pl.ds(cls + split * r0, rows, stride=split), :]
                    done += 1
                    if done % tiles_per_stage == 0:
                        yield

    def dil1_steps():
        for i in range(seq // (bq * group)):
            qs, kws, vws, biases, i0s = [], [], [], [], []
            for j in range(group):
                i0 = (i * group + j) * bq
                ws, pos = window(i0, seq)
                i0s.append(i0)
                qs.append(q_ref[0, pl.ds(i0, bq), :])
                kws.append(k_ref[0, pl.ds(ws, width), :])
                vws.append(v_ref[0, pl.ds(ws, width), :])
                biases.append(bias_scr[0, pos])
            accs, ms, ls = _attn_blocks(qs, kws, vws, biases)
            for j in range(group):
                acc1[pl.ds(i0s[j], bq), :] = accs[j]
                m1[pl.ds(i0s[j], bq), :] = rep(ms[j])
                l1[pl.ds(i0s[j], bq), :] = rep(ls[j])
            yield

    n_tiles = 3 * (seq // rows + split * (n4 // rows))
    _interleave(dil1_steps(), split_classes(n_tiles // (seq // (bq * group))))

    per_step = group // split

    def dil4(i, carry):
        qs, kws, vws, biases, q_rows = [], [], [], [], []
        for j in range(per_step):
            i0 = pl.multiple_of((i * per_step + j) * bq, bq)
            ws, pos = window(i0, n4)
            for c in range(split):
                q_rows.append((c, pl.ds(i0, bq)))
                qs.append(xq[c, pl.ds(i0, bq), :].astype(_BF16))
                kws.append(xk[c, pl.ds(ws, width), :].astype(_BF16))
                vws.append(xv[c, pl.ds(ws, width), :].astype(_BF16))
                biases.append(bias_scr[1, pos])
        accs, ms, ls = _attn_blocks(qs, kws, vws, biases)
        for (c, rows_), acc, m, l in zip(q_rows, accs, ms, ls):
            acc4[c, rows_, :] = acc
            m4[c, rows_, :] = rep(m)
            l4[c, rows_, :] = rep(l)
        return carry

    lax.fori_loop(0, n4 // (bq * per_step), dil4, 0)

    def dil16(i, carry):
        qs, kws, vws, biases, q_rows = [], [], [], [], []
        for j in range(per_step):
            blk = i * per_step + j
            sub = blk // (n16 // bq)
            i0 = pl.multiple_of((blk % (n16 // bq)) * bq, bq)
            ws, pos = window(i0, n16)
            k_rows = pl.ds(sub + split * ws, width, stride=split)
            for c in range(split):
                q_rows.append((c, pl.ds(sub + split * i0, bq, stride=split)))
                qs.append(xq[c, q_rows[-1][1], :].astype(_BF16))
                kws.append(xk[c, k_rows, :].astype(_BF16))
                vws.append(xv[c, k_rows, :].astype(_BF16))
                biases.append(bias_scr[2, pos])
        accs, ms, ls = _attn_blocks(qs, kws, vws, biases)
        for (c, rows_), acc_b, m_b, l_b in zip(q_rows, accs, ms, ls):
            acc, m, l = _attn_merge(acc4[c, rows_, :], m4[c, rows_, :], l4[c, rows_, :], acc_b, m_b, l_b)
            acc4[c, rows_, :] = acc
            m4[c, rows_, :] = m
            l4[c, rows_, :] = l
        return carry

    lax.fori_loop(0, split * (n16 // bq) // per_step, dil16, 0)

    def finish(r, carry):
        r0 = pl.multiple_of(r * bq, bq)
        merged = []
        for cls in range(split):
            nat_rows = pl.ds(cls + split * r0, bq, stride=split)
            acc, _, l = _attn_merge(acc1[nat_rows, :], m1[nat_rows, :], l1[nat_rows, :],
                                    acc4[cls, pl.ds(r0, bq), :], m4[cls, pl.ds(r0, bq), :], l4[cls, pl.ds(r0, bq), :])
            merged.append((nat_rows, acc / l))
        for nat_rows, out in merged:
            nat[nat_rows, :] = out
        sl = pl.ds(pl.multiple_of(split * r0, split * bq), split * bq)
        o_ref[0, sl, :] = nat[sl, :].astype(o_ref.dtype)
        return carry

    lax.fori_loop(0, n4 // bq, finish, 0)


def _dilated_attention(qkv, slopes):
    bsz, seq, _ = qkv.shape
    h = ATT_HEADS
    n4 = seq // ATT_SPLIT
    n16 = n4 // ATT_SPLIT
    assert n16 % ATT_BQ == 0 and n16 >= ATT_WIDTH_KEYS and seq % (ATT_BQ * ATT_GROUP) == 0
    assert ATT_GROUP % ATT_SPLIT == 0 and (ATT_SPLIT * n16 // ATT_BQ) % (ATT_GROUP // ATT_SPLIT) == 0
    col = lambda off: pl.BlockSpec((1, seq, HEAD_DIM), functools.partial(lambda b, hh, o: (b, 0, o + hh), o=off))
    slab = pltpu.VMEM((ATT_SPLIT, n4, HEAD_DIM), _F32)
    nat = pltpu.VMEM((seq, HEAD_DIM), _F32)
    return pl.pallas_call(
        _attn_kernel,
        grid=(bsz, h),
        in_specs=[pl.BlockSpec((1, 1, LANES), lambda b, hh: (hh, 0, 0)), col(0), col(h), col(2 * h)],
        out_specs=col(0),
        out_shape=jax.ShapeDtypeStruct((bsz, seq, ATT_WIDTH), _BF16),
        scratch_shapes=[nat, slab, slab, slab, slab, slab, slab, nat, nat, nat,
                        pltpu.VMEM((len(DILATED_PATTERNS), 3, ATT_BQ, ATT_WIDTH_KEYS), _F32)],
        compiler_params=_params("arbitrary", "arbitrary"),
        name="dilated_attention",
    )(slopes, qkv, qkv, qkv)


FFN_EPILOGUE_ROWS = 32

def _ffn_up_kernel(a_ref, wg_ref, wu_ref, cg_ref, cu_ref, wd_ref, o_ref, wd16_ref, wcat, y_even, y_odd, *,
                   n_tiles, tiles_per_seq, halo):
    s = pl.program_id(0)
    n_steps = pl.num_programs(0) - 1
    tn = wg_ref.shape[1]
    tm = o_ref.shape[0]
    wd16_ref[...] = wd_ref[...].astype(wd16_ref.dtype)

    @pl.when(jnp.logical_and(s % n_tiles == 0, s < n_steps))
    def _():
        _cast_rows(wg_ref, wcat.at[:, pl.ds(0, tn)])
        _cast_rows(wu_ref, wcat.at[:, pl.ds(tn, tn)])

    @pl.when(s == 0)
    def _():
        y_odd[...] = jnp.zeros(y_odd.shape, y_odd.dtype)

    tile = jnp.maximum(s - 1, 0) % n_tiles
    first = (tile % tiles_per_seq) == 0
    last = (tile % tiles_per_seq) == tiles_per_seq - 1

    def step(y_new, y_old):
        cw = jnp.concatenate([cg_ref[...], cu_ref[...]], axis=1)
        rows = FFN_EPILOGUE_ROWS
        n_chunks = tm // rows

        def finish(c):
            r0 = halo + c * rows
            y = y_old[pl.ds(r0, rows), :]
            prev_row = y_old[pl.ds(r0 - 8, 8), :][7:8, :]
            next_row = y_old[pl.ds(r0 + rows, 8), :][0:1, :]
            if c == 0:
                prev_row = jnp.where(first, 0.0, prev_row)
            if c == n_chunks - 1:
                next_row = jnp.where(last, 0.0, next_row)
            ym1, yp1 = _shift_rows(y, prev_row, next_row)
            conv = ym1 * cw[0:1, :] + y * cw[1:2, :] + yp1 * cw[2:3, :]
            gate = conv[:, :tn]
            up = conv[:, tn:]
            o_ref[pl.ds(c * rows, rows), :] = (gate * _sigmoid(gate) * up).astype(o_ref.dtype)

        proj = _dot(a_ref[...], wcat[...])
        for c in range(n_chunks):
            finish(c)
        y_new[...] = proj

    @pl.when(s % 2 == 0)
    def _():
        step(y_even, y_odd)

    @pl.when(s % 2 == 1)
    def _():
        step(y_odd, y_even)


def _ffn_up(u_pad, w_up, conv_w, w_down, seq, pad_rows, tm=1024, tn=256):
    m, k = u_pad.shape[0] - 2 * pad_rows, u_pad.shape[1]
    ffn = w_up.shape[1] // 2
    halo = BF16_ROWS
    assert ffn % tn == 0 and seq % tm == 0 and pad_rows >= halo
    nj = ffn // tn
    n_tiles = m // tm
    n_steps = nj * n_tiles
    wd_rows = w_down.shape[0] // n_steps
    assert wd_rows * n_steps == w_down.shape[0] and wd_rows % BF16_ROWS == 0
    wd_spec = pl.BlockSpec((wd_rows, w_down.shape[1]), lambda s: (jnp.minimum(s, n_steps - 1), 0))
    assert tm % halo == 0 and pad_rows % halo == 0
    col = lambda s: jnp.minimum(s // n_tiles, nj - 1)
    col_prev = lambda s: jnp.maximum(s - 1, 0) // n_tiles
    a_spec = pl.BlockSpec((pl.Element(tm + 2 * halo), pl.Element(k)),
                          lambda s: (((s % n_tiles) * (tm // halo) + (pad_rows // halo - 1)) * halo, 0))
    return pl.pallas_call(
        functools.partial(_ffn_up_kernel, n_tiles=n_tiles, tiles_per_seq=seq // tm, halo=halo),
        grid=(n_steps + 1,),
        in_specs=[a_spec,
                  pl.BlockSpec((k, tn), lambda s: (0, col(s))),
                  pl.BlockSpec((k, tn), lambda s: (0, col(s) + nj)),
                  pl.BlockSpec((3, tn), lambda s: (0, col_prev(s))),
                  pl.BlockSpec((3, tn), lambda s: (0, col_prev(s) + nj)),
                  wd_spec],
        out_specs=[pl.BlockSpec((tm, tn), lambda s: (jnp.maximum(s - 1, 0) % n_tiles, col_prev(s))), wd_spec],
        out_shape=[jax.ShapeDtypeStruct((m, ffn), _BF16), jax.ShapeDtypeStruct(w_down.shape, _BF16)],
        scratch_shapes=[pltpu.VMEM((k, 2 * tn), _BF16)] + [pltpu.VMEM((tm + 2 * halo, 2 * tn), _F32)] * 2,
        compiler_params=_params("arbitrary"),
        name="ffn_up_conv_glu",
    )(u_pad, w_up, w_up, conv_w, conv_w, w_down)


def _ple_kernel(a_ref, wg_ref, p_ref, wp_ref, h_ref, o_ref, wgb, wpb):
    @pl.when(pl.program_id(1) == 0)
    def _():
        _cast_rows(wg_ref, wgb)
        _cast_rows(wp_ref, wpb)

    gate = _sigmoid(_dot(a_ref[...], wgb[...]))
    emb = _dot(p_ref[...], wpb[...])
    o_ref[...] = h_ref[...] + gate * emb


def _ple(u, w_gate, p, w_proj, h, *, tm, tn):
    m, k = u.shape
    n = w_gate.shape[1]
    kp = p.shape[1]
    return pl.pallas_call(
        _ple_kernel,
        grid=(n // tn, m // tm),
        in_specs=[pl.BlockSpec((tm, k), lambda j, i: (i, 0)),
                  pl.BlockSpec((k, tn), lambda j, i: (0, j)),
                  pl.BlockSpec((tm, kp), lambda j, i: (i, 0)),
                  pl.BlockSpec((kp, tn), lambda j, i: (0, j)),
                  pl.BlockSpec((tm, tn), lambda j, i: (i, j))],
        out_specs=pl.BlockSpec((tm, tn), lambda j, i: (i, j)),
        out_shape=jax.ShapeDtypeStruct((m, n), _F32),
        scratch_shapes=[pltpu.VMEM((k, tn), _BF16), pltpu.VMEM((kp, tn), _BF16)],
        compiler_params=_params("arbitrary", "arbitrary"),
        name="ple_gate",
    )(u, w_gate, p, w_proj, h)


def kernel(x, p, attn_norm, w_in, gdn_conv, gdn_a_log, gdn_dt_bias, gdn_out_norm, w_out, ffn_norm, w_up,
           ffn_conv, w_down, ple_norm, w_ple_gate, w_ple_proj, final_norm):
    bsz, seq, d_model = x.shape
    depth = w_in.shape[0]
    m = bsz * seq
    qkv_cols = 3 * GDN_WIDTH
    ab_off = qkv_cols + GDN_WIDTH
    att_off = ab_off + 4 * GDN_HEADS
    slopes = jnp.exp2(-ALIBI_MAX_BIAS * (jnp.arange(ATT_HEADS, dtype=_F32) + 1.0) / ATT_HEADS)
    slopes = jnp.broadcast_to(slopes[:, None, None], (ATT_HEADS, 1, LANES))

    h = x.reshape(m, d_model)
    for i in range(depth):
        u = _rmsnorm(h, attn_norm[i], _BF16)
        wt = jnp.swapaxes(w_in[i], 0, 1)
        tm, tn = MM_TM, MM_TN
        gdn_qkv = _matmul_nt(u, wt, n_cols=qkv_cols, row_off=0, tm=tm, tn=tn, out_dtype=_F32, name="proj_gdn_qkv")
        gdn_z = _matmul_nt(u, wt, n_cols=GDN_WIDTH, row_off=qkv_cols, tm=tm, tn=tn, out_dtype=_BF16,
                           name="proj_gdn_z")
        ab = _matmul_nt(u, wt, n_cols=LANES, row_off=ab_off, tm=tm, tn=LANES, out_dtype=_F32, name="proj_gdn_ab")
        att_qkv = _matmul_nt(u, wt, n_cols=3 * ATT_WIDTH, row_off=att_off, tm=tm, tn=tn, out_dtype=_BF16,
                             name="proj_att_qkv")

        slab = _gdn_gates(ab, gdn_a_log[i], gdn_dt_bias[i]).reshape(bsz, seq, LANES)
        o_gdn = _gdn_mixer(gdn_qkv.reshape(bsz, seq, qkv_cols), gdn_z.reshape(bsz, seq, GDN_WIDTH), slab,
                           gdn_conv[i], gdn_out_norm[i]).reshape(m, GDN_WIDTH)

        att3 = att_qkv.reshape(bsz, seq, 3 * ATT_WIDTH)
        o_att = _dilated_attention(att3, slopes).reshape(m, ATT_WIDTH)

        h = _matmul([o_gdn, o_att], w_out[i], n_cols=d_model, tm=tm, tn=tn, out_dtype=_F32, residual=h,
                    name="out_proj")

        u_pad = _rmsnorm(h, ffn_norm[i], _BF16, pad_tiles=1)
        act, w_down16 = _ffn_up(u_pad, w_up[i], ffn_conv[i], w_down[i], seq, pad_rows=NORM_TM)
        h = _matmul([act], w_down16, n_cols=d_model, tm=MM_TM_LONG_K, tn=tn, out_dtype=_F32,
                    residual=h, name="ffn_down")

        u = _rmsnorm(h, ple_norm[i], _BF16)
        h = _ple(u, w_ple_gate[i], p[i].reshape(m, -1).astype(_BF16), w_ple_proj[i], h, tm=tm, tn=tn)
    return _rmsnorm(h, final_norm, x.dtype).reshape(bsz, seq, d_model)
```
